```python
import jax, jax.numpy as jnp
from jax import lax
import numpy as np

D_MODEL = 4096
BATCH = 2
SEQ = 8192
DEPTH = 2

MEM_LEN = 256
CROSS_HEADS = 4
CROSS_HD = D_MODEL // CROSS_HEADS

MIX_WIDTH = D_MODEL
CONV_WIDTH = D_MODEL // 4
CONV_K = 3
GLA_HEADS = 4
GLA_DV = (D_MODEL // 4) // GLA_HEADS
GLA_DK = GLA_DV // 2
GLA_RANK = 16
GLA_TAU = 16.0
GLA_CHUNK = 64
FOX_HD = 128
FOX_HEADS = (D_MODEL // 2) // FOX_HD
FOX_QBLOCK = 128

IN_SIZES = (CONV_WIDTH, CONV_WIDTH, CONV_WIDTH,
            GLA_HEADS * GLA_DK, GLA_HEADS * GLA_DK,
            GLA_HEADS * GLA_DV, GLA_HEADS * GLA_DV, GLA_RANK,
            FOX_HEADS * FOX_HD, FOX_HEADS * FOX_HD, FOX_HEADS * FOX_HD,
            FOX_HEADS)
IN_COLS = 3 * CONV_WIDTH + 2 * GLA_HEADS * GLA_DK + 2 * GLA_HEADS * GLA_DV + GLA_RANK + 3 * FOX_HEADS * FOX_HD + FOX_HEADS

N_GROUPS = 4
EXPERTS_PER_GROUP = 8
N_EXPERTS = N_GROUPS * EXPERTS_PER_GROUP
TOP_K = 2
D_EXPERT = D_MODEL // 8
ROW_BLOCK = 128

RMS_EPS = 1e-6

kernel_name = 'hymba_conv_gla_fox_hmoe_trunk'


def rmsnorm(x, w):
    xf = x.astype(jnp.float32)
    y = xf * lax.rsqrt(jnp.mean(xf * xf, axis=-1, keepdims=True) + RMS_EPS)
    return (y * w.astype(jnp.float32)).astype(x.dtype)


def causal_depthwise_conv(u, conv_w):
    rhs = conv_w.reshape(CONV_K, 1, u.shape[-1]).astype(u.dtype)
    return lax.conv_general_dilated(u, rhs, window_strides=(1,), padding=[(CONV_K - 1, 0)],
                                    dimension_numbers=('NWC', 'WIO', 'NWC'),
                                    feature_group_count=u.shape[-1])


def gated_linear_attention(q, k, v, g, lowrank, wf2, bf, norm_w):
    f32 = jnp.float32
    bsz, seq, _ = q.shape
    n_chunks = seq // GLA_CHUNK
    log_a = jax.nn.log_sigmoid(lowrank.astype(f32) @ wf2.astype(f32) + bf.astype(f32)) / GLA_TAU

    def chunked(t, d):
        return t.astype(f32).reshape(bsz, n_chunks, GLA_CHUNK, GLA_HEADS, d)

    qc = chunked(q, GLA_DK) * (GLA_DK ** -0.5)
    kc = chunked(k, GLA_DK)
    vc = chunked(v, GLA_DV)
    b = jnp.cumsum(chunked(log_a, GLA_DK), axis=2)
    b_last = b[:, :, -1]
    q_in = qc * jnp.exp(b)
    k_in = kc * jnp.exp(-b)
    k_out = kc * jnp.exp(b_last[:, :, None] - b)

    causal = jnp.tril(jnp.ones((GLA_CHUNK, GLA_CHUNK), dtype=bool))
    att = jnp.where(causal, jnp.einsum('bnthk,bnshk->bnhts', q_in, k_in), 0.0)
    o_intra = jnp.einsum('bnhts,bnshv->bnthv', att, vc)
    kv = jnp.einsum('bnshk,bnshv->bnhkv', k_out, vc)

    def step(state, inp):
        decay, kv_n = inp
        return state * decay[..., None] + kv_n, state

    state0 = jnp.zeros((bsz, GLA_HEADS, GLA_DK, GLA_DV), f32)
    _, states = lax.scan(step, state0, (jnp.exp(b_last).transpose(1, 0, 2, 3), kv.transpose(1, 0, 2, 3, 4)))
    o_inter = jnp.einsum('bnthk,nbhkv->bnthv', q_in, states)
    o = (o_intra + o_inter).reshape(bsz, seq, GLA_HEADS, GLA_DV)
    o = o * lax.rsqrt(jnp.mean(o * o, axis=-1, keepdims=True) + RMS_EPS) * norm_w.astype(f32)
    o = o.reshape(bsz, seq, GLA_HEADS * GLA_DV) * jax.nn.silu(g.astype(f32))
    return o.astype(q.dtype)


def forgetting_attention(q, k, v, f_logit, fox_bf):
    f32 = jnp.float32
    bsz, seq, heads, hd = q.shape
    log_f = jax.nn.log_sigmoid((f_logit + fox_bf).astype(f32))
    c = jnp.cumsum(log_f, axis=1).transpose(0, 2, 1)
    key_pos = jnp.arange(seq)
    scale = hd ** -0.5

    def block(i):
        start = i * FOX_QBLOCK
        qb = lax.dynamic_slice_in_dim(q, start, FOX_QBLOCK, axis=1)
        cb = lax.dynamic_slice_in_dim(c, start, FOX_QBLOCK, axis=2)
        s = jnp.einsum('bqhd,bkhd->bhqk', qb, k, preferred_element_type=f32) * scale
        s = s + cb[..., None] - c[:, :, None, :]
        q_pos = start + jnp.arange(FOX_QBLOCK)
        s = jnp.where(key_pos[None, :] <= q_pos[:, None], s, -jnp.inf)
        p = jax.nn.softmax(s, axis=-1).astype(v.dtype)
        return jnp.einsum('bhqk,bkhd->bqhd', p, v)

    out = lax.map(block, jnp.arange(seq // FOX_QBLOCK))
    return out.transpose(1, 0, 2, 3, 4).reshape(bsz, seq, heads * hd)


def hybrid_mixer(xn, w_in, conv_w, gla_wf2, gla_bf, gla_norm_w, fox_bf, w_out):
    bsz, seq, _ = xn.shape
    proj = xn @ w_in
    cuts, acc = [], 0
    for size in IN_SIZES[:-1]:
        acc += size
        cuts.append(acc)
    (c_h, c_b, c_c, g_q, g_k, g_v, g_g, g_lr, f_q, f_k, f_v, f_f) = jnp.split(proj, cuts, axis=-1)

    y_conv = c_b * causal_depthwise_conv(c_c * c_h, conv_w)
    y_gla = gated_linear_attention(g_q, g_k, g_v, g_g, g_lr, gla_wf2, gla_bf, gla_norm_w)
    heads = lambda t: t.reshape(bsz, seq, FOX_HEADS, FOX_HD)
    y_fox = forgetting_attention(heads(f_q), heads(f_k), heads(f_v), f_f, fox_bf)
    return jnp.concatenate([y_conv, y_gla, y_fox], axis=-1) @ w_out


def memory_cross_attention(xn, mem_n, w_cq, w_ck, w_cv, w_co):
    bsz, seq, _ = xn.shape
    m = mem_n.shape[1]
    q = (xn @ w_cq).reshape(bsz, seq, CROSS_HEADS, CROSS_HD)
    k = (mem_n @ w_ck).reshape(bsz, m, CROSS_HEADS, CROSS_HD)
    v = (mem_n @ w_cv).reshape(bsz, m, CROSS_HEADS, CROSS_HD)
    s = jnp.einsum('bshd,bmhd->bhsm', q, k, preferred_element_type=jnp.float32) * (CROSS_HD ** -0.5)
    p = jax.nn.softmax(s, axis=-1).astype(v.dtype)
    o = jnp.einsum('bhsm,bmhd->bshd', p, v).reshape(bsz, seq, D_MODEL)
    return o @ w_co


def hierarchical_moe(xn, w_group, b_group, w_router, b_router, w_gate, w_up, w_down):
    f32 = jnp.float32
    bsz, seq, d = xn.shape
    t = bsz * seq
    xf = xn.reshape(t, d)
    group_logits = (xf @ w_group + b_group).astype(f32)
    group = jnp.argmax(group_logits, axis=-1)
    p_group = jnp.take_along_axis(jax.nn.softmax(group_logits, axis=-1), group[:, None], axis=-1)[:, 0]
    exp_logits = (xf @ w_router + b_router).astype(f32).reshape(t, N_GROUPS, EXPERTS_PER_GROUP)
    in_group = jnp.take_along_axis(exp_logits, group[:, None, None], axis=1)[:, 0]
    top_p, top_e = lax.top_k(jax.nn.softmax(in_group, axis=-1), TOP_K)
    gate = p_group[:, None] * top_p / jnp.sum(top_p, axis=-1, keepdims=True)
    eid = (group[:, None] * EXPERTS_PER_GROUP + top_e).reshape(-1)
    gate_flat = gate.reshape(-1).astype(xf.dtype)
    n_assign = t * TOP_K

    order = jnp.argsort(eid)
    s_eid = eid[order]
    s_tok = order // TOP_K
    s_gate = gate_flat[order]
    counts = jnp.bincount(eid, length=N_EXPERTS)
    starts = jnp.cumsum(counts) - counts
    padded = (counts + ROW_BLOCK - 1) // ROW_BLOCK * ROW_BLOCK
    padded_end = jnp.cumsum(padded)
    padded_start = padded_end - padded
    dest = padded_start[s_eid] + (jnp.arange(n_assign) - starts[s_eid])
    n_rows = n_assign + N_EXPERTS * ROW_BLOCK
    n_blocks = n_rows // ROW_BLOCK
    row_tok = jnp.full((n_rows,), t, jnp.int32).at[dest].set(s_tok.astype(jnp.int32))
    row_gate = jnp.zeros((n_rows,), xf.dtype).at[dest].set(s_gate)
    block_expert = jnp.minimum(jnp.searchsorted(padded_end, jnp.arange(n_blocks) * ROW_BLOCK, side='right'), N_EXPERTS - 1)
    xf_ext = jnp.concatenate([xf, jnp.zeros((1, d), xf.dtype)], axis=0)

    def step(acc_out, inp):
        e, toks, gts = inp
        xb = xf_ext[toks]
        h = jax.nn.silu(xb @ w_gate[e]) * (xb @ w_up[e])
        yb = (h @ w_down[e]) * gts[:, None]
        return acc_out.at[toks].add(yb), None

    out0 = jnp.zeros((t + 1, d), xf.dtype)
    out, _ = lax.scan(step, out0, (block_expert, row_tok.reshape(n_blocks, ROW_BLOCK), row_gate.reshape(n_blocks, ROW_BLOCK)))
    return out[:t].reshape(bsz, seq, d)


def setup_inputs(seed: int = 0) -> dict:
    key = jax.random.key(seed)
    ks = jax.random.split(key, 26)
    L = DEPTH

    def nrm(k, shape, fan_in):
        return jax.random.normal(k, shape, jnp.float32) * (fan_in ** -0.5)

    def gain(k, shape):
        return 1.0 + 0.01 * jax.random.normal(k, shape, jnp.float32)

    return {
        'x': jax.random.normal(ks[0], (BATCH, SEQ, D_MODEL), jnp.float32),
        'mem': jax.random.normal(ks[1], (BATCH, MEM_LEN, D_MODEL), jnp.float32),
        'norm_mix_w': gain(ks[2], (L, D_MODEL)),
        'w_in': nrm(ks[3], (L, D_MODEL, IN_COLS), D_MODEL),
        'conv_w': nrm(ks[4], (L, CONV_K, CONV_WIDTH), CONV_K),
        'gla_wf2': nrm(ks[5], (L, GLA_RANK, GLA_HEADS * GLA_DK), GLA_RANK),
        'gla_bf': 0.1 * jax.random.normal(ks[6], (L, GLA_HEADS * GLA_DK), jnp.float32),
        'gla_norm_w': gain(ks[7], (L, GLA_HEADS, GLA_DV)),
        'fox_bf': 2.0 + 0.1 * jax.random.normal(ks[8], (L, FOX_HEADS), jnp.float32),
        'w_out': nrm(ks[9], (L, MIX_WIDTH, D_MODEL), MIX_WIDTH),
        'norm_cross_w': gain(ks[10], (L, D_MODEL)),
        'mem_norm_w': gain(ks[11], (D_MODEL,)),
        'w_cq': nrm(ks[12], (L, D_MODEL, D_MODEL), D_MODEL),
        'w_ck': nrm(ks[13], (L, D_MODEL, D_MODEL), D_MODEL),
        'w_cv': nrm(ks[14], (L, D_MODEL, D_MODEL), D_MODEL),
        'w_co': nrm(ks[15], (L, D_MODEL, D_MODEL), D_MODEL),
        'norm_ffn_w': gain(ks[16], (L, D_MODEL)),
        'w_group': nrm(ks[17], (L, D_MODEL, N_GROUPS), D_MODEL),
        'b_group': 0.01 * jax.random.normal(ks[18], (L, N_GROUPS), jnp.float32),
        'w_router': nrm(ks[19], (L, D_MODEL, N_EXPERTS), D_MODEL),
        'b_router': 0.01 * jax.random.normal(ks[20], (L, N_EXPERTS), jnp.float32),
        'w_expert_gate': nrm(ks[21], (L, N_EXPERTS, D_MODEL, D_EXPERT), D_MODEL),
        'w_expert_up': nrm(ks[22], (L, N_EXPERTS, D_MODEL, D_EXPERT), D_MODEL),
        'w_expert_down': nrm(ks[23], (L, N_EXPERTS, D_EXPERT, D_MODEL), D_EXPERT),
        'final_norm_w': gain(ks[24], (D_MODEL,)),
    }


def reference(x, mem, norm_mix_w, w_in, conv_w, gla_wf2, gla_bf, gla_norm_w, fox_bf, w_out,
              norm_cross_w, mem_norm_w, w_cq, w_ck, w_cv, w_co, norm_ffn_w, w_group, b_group,
              w_router, b_router, w_expert_gate, w_expert_up, w_expert_down, final_norm_w):
    mem_n = rmsnorm(mem, mem_norm_w)
    h = x
    for l in range(DEPTH):
        h = h + hybrid_mixer(rmsnorm(h, norm_mix_w[l]), w_in[l], conv_w[l], gla_wf2[l], gla_bf[l],
                             gla_norm_w[l], fox_bf[l], w_out[l])
        h = h + memory_cross_attention(rmsnorm(h, norm_cross_w[l]), mem_n, w_cq[l], w_ck[l], w_cv[l], w_co[l])
        h = h + hierarchical_moe(rmsnorm(h, norm_ffn_w[l]), w_group[l], b_group[l], w_router[l], b_router[l],
                                 w_expert_gate[l], w_expert_up[l], w_expert_down[l])
    return rmsnorm(h, final_norm_w)
```

```python
import functools

import jax
import jax.numpy as jnp
from jax import lax
from jax.experimental import pallas as pl
from jax.experimental.pallas import tpu as pltpu

F32 = jnp.float32
BF16 = jnp.bfloat16

D_MODEL = 4096
CONV_WIDTH = 1024
GLA_HEADS = 4
GLA_DK = 128
GLA_DV = 256
GLA_RANK = 16
GLA_TAU = 16.0
GLA_CHUNK = 64
FOX_HEADS = 16
FOX_HD = 128
CROSS_HEADS = 4
CROSS_HD = 1024
N_GROUPS = 4
EXPERTS_PER_GROUP = 8
N_EXPERTS = 32
TOP_K = 2
D_EXPERT = 512
RMS_EPS = 1e-6

LANES = 128
BF16_SUBLANES = 16
VMEM_LIMIT = 52 * 1024 * 1024

COL_CONV_H, COL_CONV_B, COL_CONV_C = 0, 1024, 2048
COL_GLA_Q, COL_GLA_K, COL_GLA_V, COL_GLA_G = 3072, 3584, 4096, 5120
COL_FOX_Q, COL_FOX_K, COL_FOX_V = 6144, 8192, 10240
MAIN_COLS = 12288
SMALL_LR, SMALL_F = 0, 16


def _params(n_axes):
    return pltpu.CompilerParams(dimension_semantics=("arbitrary",) * n_axes, vmem_limit_bytes=VMEM_LIMIT)


def _log_sigmoid(x):
    return jnp.minimum(x, 0.0) - jnp.log1p(jnp.exp(-jnp.abs(x)))


def _rms_scale(x, nw):
    ms = jnp.mean(x * x, axis=-1, keepdims=True)
    return x * lax.rsqrt(ms + RMS_EPS) * nw


def _norm_matmul_body(x_ref, nw_ref, w_ref, o_ref, xn_ref):
    @pl.when(pl.program_id(1) == 0)
    def _():
        xn_ref[...] = _rms_scale(x_ref[...], nw_ref[...]).astype(BF16)

    o_ref[...] = jnp.dot(xn_ref[...], w_ref[...], preferred_element_type=F32).astype(o_ref.dtype)


def norm_matmul(x, nw, w, *, tm=512, tn=512):
    m, k = x.shape
    n = w.shape[1]
    tm, tn = min(tm, m), min(tn, n)
    return pl.pallas_call(
        _norm_matmul_body,
        grid=(m // tm, n // tn),
        in_specs=[pl.BlockSpec((tm, k), lambda i, j: (i, 0)),
                  pl.BlockSpec((1, k), lambda i, j: (0, 0)),
                  pl.BlockSpec((k, tn), lambda i, j: (0, j))],
        out_specs=pl.BlockSpec((tm, tn), lambda i, j: (i, j)),
        out_shape=jax.ShapeDtypeStruct((m, n), BF16),
        scratch_shapes=[pltpu.VMEM((tm, k), BF16)],
        compiler_params=_params(2),
        name="norm_matmul",
    )(x, nw.reshape(1, k), w)


def _in_proj_body(x_ref, nw_ref, w_ref, ws_ref, o_ref, os_ref, xn_ref):
    @pl.when(pl.program_id(1) == 0)
    def _():
        xn = _rms_scale(x_ref[...], nw_ref[...]).astype(BF16)
        xn_ref[...] = xn
        os_ref[...] = jnp.dot(xn, ws_ref[...], preferred_element_type=F32)

    o_ref[...] = jnp.dot(xn_ref[...], w_ref[...], preferred_element_type=F32).astype(o_ref.dtype)


def in_proj(x, nw, w_main, w_small, *, tm=512, tn=512):
    m, k = x.shape
    n = w_main.shape[1]
    tm = min(tm, m)
    return pl.pallas_call(
        _in_proj_body,
        grid=(m // tm, n // tn),
        in_specs=[pl.BlockSpec((tm, k), lambda i, j: (i, 0)),
                  pl.BlockSpec((1, k), lambda i, j: (0, 0)),
                  pl.BlockSpec((k, tn), lambda i, j: (0, j)),
                  pl.BlockSpec((k, LANES), lambda i, j: (0, 0))],
        out_specs=[pl.BlockSpec((tm, tn), lambda i, j: (i, j)),
                   pl.BlockSpec((tm, LANES), lambda i, j: (i, 0))],
        out_shape=[jax.ShapeDtypeStruct((m, n), BF16), jax.ShapeDtypeStruct((m, LANES), F32)],
        scratch_shapes=[pltpu.VMEM((tm, k), BF16)],
        compiler_params=_params(2),
        name="in_proj",
    )(x, nw.reshape(1, k), w_main, w_small)


def _conv_body(h_ref, b_ref, c_ref, ph_ref, pc_ref, w_ref, o_ref):
    i = pl.program_id(1)
    u = c_ref[...].astype(F32) * h_ref[...].astype(F32)
    up = pc_ref[...].astype(F32) * ph_ref[...].astype(F32)
    up = jnp.where(i > 0, up, 0.0)
    last, last2 = up[BF16_SUBLANES - 1:BF16_SUBLANES], up[BF16_SUBLANES - 2:BF16_SUBLANES - 1]
    row = lax.broadcasted_iota(jnp.int32, u.shape, 0)
    u1 = jnp.where(row == 0, last, pltpu.roll(u, 1, 0))
    u2 = jnp.where(row == 0, last2, jnp.where(row == 1, last, pltpu.roll(u, 2, 0)))
    w = w_ref[...]
    z = w[0:1] * u2 + w[1:2] * u1 + w[2:3] * u
    o_ref[...] = (b_ref[...].astype(F32) * z).astype(BF16)


def gated_conv(proj, conv_w, bsz, seq, *, ts=512):
    ts = min(ts, seq)
    ns = seq // ts
    cw = CONV_WIDTH
    cb = lambda col: pl.BlockSpec((ts, cw), lambda b, i: (b * ns + i, col // cw))
    pb = lambda col: pl.BlockSpec(
        (BF16_SUBLANES, cw),
        lambda b, i: (jnp.maximum((b * seq + i * ts) // BF16_SUBLANES - 1, 0), col // cw))
    return pl.pallas_call(
        _conv_body,
        grid=(bsz, ns),
        in_specs=[cb(COL_CONV_H), cb(COL_CONV_B), cb(COL_CONV_C), pb(COL_CONV_H), pb(COL_CONV_C),
                  pl.BlockSpec(conv_w.shape, lambda b, i: (0, 0))],
        out_specs=pl.BlockSpec((ts, cw), lambda b, i: (b * ns + i, 0)),
        out_shape=jax.ShapeDtypeStruct((bsz * seq, cw), BF16),
        compiler_params=_params(2),
        name="gated_conv",
    )(proj, proj, proj, proj, proj, conv_w)


def _fox_c_body(x_ref, bias_ref, o_ref, carry_ref):
    @pl.when(pl.program_id(1) == 0)
    def _():
        carry_ref[...] = jnp.zeros_like(carry_ref)

    lf = _log_sigmoid(x_ref[...] + bias_ref[...])
    ts = lf.shape[0]
    row = lax.broadcasted_iota(jnp.int32, lf.shape, 0)
    k = 1
    while k < ts:
        lf = lf + jnp.where(row >= k, pltpu.roll(lf, k, 0), 0.0)
        k *= 2
    c = lf + carry_ref[...]
    o_ref[...] = c
    carry_ref[...] = c[ts - 1:ts, :]


def fox_cumulative_gate(small, bias_row, bsz, seq, *, ts=512):
    ts = min(ts, seq)
    ns = seq // ts
    return pl.pallas_call(
        _fox_c_body,
        grid=(bsz, ns),
        in_specs=[pl.BlockSpec((ts, LANES), lambda b, i: (b * ns + i, 0)),
                  pl.BlockSpec((1, LANES), lambda b, i: (0, 0))],
        out_specs=pl.BlockSpec((ts, LANES), lambda b, i: (b * ns + i, 0)),
        out_shape=jax.ShapeDtypeStruct((bsz * seq, LANES), F32),
        scratch_shapes=[pltpu.VMEM((1, LANES), F32)],
        compiler_params=_params(2),
        name="fox_cumgate",
    )(small, bias_row)


_NT = (((1,), (1,)), ((), ()))
_TN = (((0,), (0,)), ((), ()))


def _gla_body(q_ref, k_ref, v_ref, g_ref, lr_ref, wf_ref, bf_ref, nw_ref, o_ref, st_ref, b_ref):
    @pl.when(pl.program_id(1) == 0)
    def _():
        st_ref[...] = jnp.zeros_like(st_ref)

    z = jnp.dot(lr_ref[...].astype(BF16), wf_ref[...], preferred_element_type=F32) + bf_ref[...]
    la = _log_sigmoid(z) * (1.0 / GLA_TAU)
    rin = lax.broadcasted_iota(jnp.int32, la.shape, 0) & (GLA_CHUNK - 1)
    k = 1
    while k < GLA_CHUNK:
        la = la + jnp.where(rin >= k, pltpu.roll(la, k, 0), 0.0)
        k *= 2
    b_ref[...] = la

    tril = (lax.broadcasted_iota(jnp.int32, (GLA_CHUNK, GLA_CHUNK), 0)
            >= lax.broadcasted_iota(jnp.int32, (GLA_CHUNK, GLA_CHUNK), 1))

    def chunk(c, carry):
        r0 = pl.multiple_of(c * GLA_CHUNK, GLA_CHUNK)
        rows = pl.ds(r0, GLA_CHUNK)
        for h in range(GLA_HEADS):
            ks = slice(h * GLA_DK, (h + 1) * GLA_DK)
            vs = slice(h * GLA_DV, (h + 1) * GLA_DV)
            bh = b_ref[rows, ks]
            bl = bh[GLA_CHUNK - 1:GLA_CHUNK, :]
            qh = q_ref[rows, ks].astype(F32) * (GLA_DK ** -0.5)
            kh = k_ref[rows, ks].astype(F32)
            vh = v_ref[rows, vs]
            q_in = (qh * jnp.exp(bh)).astype(BF16)
            k_in = (kh * jnp.exp(-bh)).astype(BF16)
            k_out = (kh * jnp.exp(bl - bh)).astype(BF16)
            att = lax.dot_general(q_in, k_in, _NT, preferred_element_type=F32)
            att = jnp.where(tril, att, 0.0).astype(BF16)
            st = st_ref[h]
            o = (jnp.dot(att, vh, preferred_element_type=F32)
                 + lax.dot_general(q_in, st.astype(BF16), _NT, preferred_element_type=F32))
            kv = lax.dot_general(vh, k_out, _TN, preferred_element_type=F32)
            st_ref[h] = st * jnp.exp(bl) + kv
            on = _rms_scale(o, nw_ref[:, vs])
            gg = g_ref[rows, vs].astype(F32)
            o_ref[rows, vs] = (on * (gg * jax.nn.sigmoid(gg))).astype(BF16)
        return carry

    lax.fori_loop(0, b_ref.shape[0] // GLA_CHUNK, chunk, 0)


def gated_linear_attention(proj, small, wf_pad, bf_row, nw_row, bsz, seq, *, tg=512):
    tg = min(tg, seq)
    ns = seq // tg
    qk_w, v_w = GLA_HEADS * GLA_DK, GLA_HEADS * GLA_DV
    rb = lambda col, w: pl.BlockSpec((tg, w), lambda b, i: (b * ns + i, col // w))
    full = lambda a: pl.BlockSpec(a.shape, lambda b, i: (0,) * a.ndim)
    return pl.pallas_call(
        _gla_body,
        grid=(bsz, ns),
        in_specs=[rb(COL_GLA_Q, qk_w), rb(COL_GLA_K, qk_w), rb(COL_GLA_V, v_w), rb(COL_GLA_G, v_w),
                  pl.BlockSpec((tg, LANES), lambda b, i: (b * ns + i, 0)),
                  full(wf_pad), full(bf_row), full(nw_row)],
        out_specs=pl.BlockSpec((tg, v_w), lambda b, i: (b * ns + i, 0)),
        out_shape=jax.ShapeDtypeStruct((bsz * seq, v_w), BF16),
        scratch_shapes=[pltpu.VMEM((GLA_HEADS, GLA_DV, GLA_DK), F32), pltpu.VMEM((tg, qk_w), F32)],
        compiler_params=_params(2),
        name="gla",
    )(proj, proj, proj, proj, small, wf_pad, bf_row, nw_row)


def _fox_body(q_ref, k_ref, v_ref, c_ref, o_ref, m_ref, l_ref, acc_ref, *, tq):
    qi = pl.program_id(2)
    q = q_ref[...]
    c0 = c_ref[pl.ds(qi, 1), :][:, 0:1]
    m_ref[...] = jnp.full_like(m_ref, -jnp.inf)
    l_ref[...] = jnp.zeros_like(l_ref)
    acc_ref[...] = jnp.zeros_like(acc_ref)

    def block(j, masked):
        rows = pl.ds(pl.multiple_of(j * tq, tq), tq)
        s = lax.dot_general(q, k_ref[rows, :], _NT, preferred_element_type=F32) * (FOX_HD ** -0.5)
        s = s - (c_ref[pl.ds(j, 1), :] - c0)
        if masked:
            keep = (lax.broadcasted_iota(jnp.int32, s.shape, 1) <= lax.broadcasted_iota(jnp.int32, s.shape, 0))
            s = jnp.where(keep, s, -jnp.inf)
        m_prev = m_ref[...]
        m_new = jnp.maximum(m_prev, jnp.max(s, axis=-1, keepdims=True))
        alpha = jnp.exp(m_prev - m_new)
        p = jnp.exp(s - m_new)
        l_ref[...] = alpha * l_ref[...] + jnp.sum(p, axis=-1, keepdims=True)
        acc_ref[...] = alpha * acc_ref[...] + jnp.dot(p.astype(BF16), v_ref[rows, :], preferred_element_type=F32)
        m_ref[...] = m_new

    def off_diagonal(j, carry):
        block(j, False)
        return carry

    lax.fori_loop(0, qi, off_diagonal, 0)
    block(qi, True)
    o_ref[...] = (acc_ref[...] / l_ref[...]).astype(BF16)


def forgetting_attention(proj, c_blocks, bsz, seq, *, tq=512):
    tq = min(tq, seq)
    nq = seq // tq
    hd = FOX_HD
    return pl.pallas_call(
        functools.partial(_fox_body, tq=tq),
        grid=(bsz, FOX_HEADS, nq),
        in_specs=[pl.BlockSpec((tq, hd), lambda b, h, i: (b * nq + i, COL_FOX_Q // hd + h)),
                  pl.BlockSpec((seq, hd), lambda b, h, i: (b, COL_FOX_K // hd + h)),
                  pl.BlockSpec((seq, hd), lambda b, h, i: (b, COL_FOX_V // hd + h)),
                  pl.BlockSpec((None, None, nq, tq), lambda b, h, i: (b, h, 0, 0))],
        out_specs=pl.BlockSpec((tq, hd), lambda b, h, i: (b * nq + i, h)),
        out_shape=jax.ShapeDtypeStruct((bsz * seq, FOX_HEADS * hd), BF16),
        scratch_shapes=[pltpu.VMEM((tq, 1), F32), pltpu.VMEM((tq, 1), F32), pltpu.VMEM((tq, hd), F32)],
        compiler_params=_params(3),
        name="fox",
    )(proj, proj, proj, c_blocks)


def _mix_out_body(yc_ref, yg_ref, yf_ref, wc_ref, wg_ref, wf_ref, r_ref, o_ref):
    acc = jnp.dot(yc_ref[...], wc_ref[...], preferred_element_type=F32)
    acc += jnp.dot(yg_ref[...], wg_ref[...], preferred_element_type=F32)
    acc += jnp.dot(yf_ref[...], wf_ref[...], preferred_element_type=F32)
    o_ref[...] = r_ref[...] + acc


def mixer_out_proj(y_conv, y_gla, y_fox, w_out, res, *, tm=512, tn=512):
    m = res.shape[0]
    n = w_out.shape[1]
    tm = min(tm, m)
    kc, kg, kf = y_conv.shape[1], y_gla.shape[1], y_fox.shape[1]
    return pl.pallas_call(
        _mix_out_body,
        grid=(m // tm, n // tn),
        in_specs=[pl.BlockSpec((tm, kc), lambda i, j: (i, 0)),
                  pl.BlockSpec((tm, kg), lambda i, j: (i, 0)),
                  pl.BlockSpec((tm, kf), lambda i, j: (i, 0)),
                  pl.BlockSpec((kc, tn), lambda i, j: (0, j)),
                  pl.BlockSpec((kg, tn), lambda i, j: (kc // kg, j)),
                  pl.BlockSpec((kf, tn), lambda i, j: ((kc + kg) // kf, j)),
                  pl.BlockSpec((tm, tn), lambda i, j: (i, j))],
        out_specs=pl.BlockSpec((tm, tn), lambda i, j: (i, j)),
        out_shape=jax.ShapeDtypeStruct((m, n), F32),
        compiler_params=_params(2),
        name="mixer_out_proj",
    )(y_conv, y_gla, y_fox, w_out, w_out, w_out, res)


def _cross_body(q_ref, kt_ref, v_ref, w_ref, r_ref, o_ref, att_ref):
    @pl.when(pl.program_id(1) == 0)
    def _():
        for h in range(CROSS_HEADS):
            hs = slice(h * CROSS_HD, (h + 1) * CROSS_HD)
            s = jnp.dot(q_ref[:, hs], kt_ref[hs, :], preferred_element_type=F32) * (CROSS_HD ** -0.5)
            p = jnp.exp(s - jnp.max(s, axis=-1, keepdims=True))
            p = p / jnp.sum(p, axis=-1, keepdims=True)
            att_ref[:, hs] = jnp.dot(p.astype(BF16), v_ref[:, hs], preferred_element_type=F32).astype(BF16)

    o_ref[...] = r_ref[...] + jnp.dot(att_ref[...], w_ref[...], preferred_element_type=F32)


def cross_attention_out(q, k_t, v, w_co, res, seq, *, tm=512, tn=512):
    m, d = q.shape
    mem_len = v.shape[1]
    tm = min(tm, seq)
    return pl.pallas_call(
        _cross_body,
        grid=(m // tm, d // tn),
        in_specs=[pl.BlockSpec((tm, d), lambda i, j: (i, 0)),
                  pl.BlockSpec((None, d, mem_len), lambda i, j: ((i * tm) // seq, 0, 0)),
                  pl.BlockSpec((None, mem_len, d), lambda i, j: ((i * tm) // seq, 0, 0)),
                  pl.BlockSpec((d, tn), lambda i, j: (0, j)),
                  pl.BlockSpec((tm, tn), lambda i, j: (i, j))],
        out_specs=pl.BlockSpec((tm, tn), lambda i, j: (i, j)),
        out_shape=jax.ShapeDtypeStruct((m, d), F32),
        scratch_shapes=[pltpu.VMEM((tm, d), BF16)],
        compiler_params=_params(2),
        name="cross_attention_out",
    )(q, k_t, v, w_co, res)


def _router_body(x_ref, nw_ref, whi_ref, wlo_ref, xn_ref, lg_ref):
    xn = _rms_scale(x_ref[...], nw_ref[...])
    xn_ref[...] = xn
    hi = xn.astype(BF16)
    lo = (xn - hi.astype(F32)).astype(BF16)
    lg_ref[...] = (jnp.dot(hi, whi_ref[...], preferred_element_type=F32)
                   + (jnp.dot(hi, wlo_ref[...], preferred_element_type=F32)
                      + jnp.dot(lo, whi_ref[...], preferred_element_type=F32)))


def moe_router(x, nw, w_hi, w_lo, *, tm=256):
    m, k = x.shape
    tm = min(tm, m)
    return pl.pallas_call(
        _router_body,
        grid=(m // tm,),
        in_specs=[pl.BlockSpec((tm, k), lambda i: (i, 0)),
                  pl.BlockSpec((1, k), lambda i: (0, 0)),
                  pl.BlockSpec((k, LANES), lambda i: (0, 0)),
                  pl.BlockSpec((k, LANES), lambda i: (0, 0))],
        out_specs=[pl.BlockSpec((tm, k), lambda i: (i, 0)),
                   pl.BlockSpec((tm, LANES), lambda i: (i, 0))],
        out_shape=[jax.ShapeDtypeStruct((m, k), F32), jax.ShapeDtypeStruct((m, LANES), F32)],
        compiler_params=_params(1),
        name="moe_router",
    )(x, nw.reshape(1, k), w_hi, w_lo)


def _row_copy(src_hbm, src_row, dst, dst_row, sem):
    return pltpu.make_async_copy(src_hbm.at[pl.ds(src_row, 1)], dst.at[pl.ds(dst_row, 1)], sem)


def _experts_body(be_ref, tok_ref, nused_ref, x_hbm, gate_ref, wg_ref, wu_ref, wd_ref, y_ref, xbuf, sem, *, rb):
    i = pl.program_id(0)
    n_used = nused_ref[0]

    def start_gather(blk, slot):
        def body(r, carry):
            _row_copy(x_hbm, tok_ref[blk * rb + r], xbuf.at[slot], r, sem.at[slot]).start()
            return carry
        lax.fori_loop(0, rb, body, 0)

    def wait_gather(slot):
        def body(r, carry):
            _row_copy(x_hbm, 0, xbuf.at[slot], r, sem.at[slot]).wait()
            return carry
        lax.fori_loop(0, rb, body, 0)

    @pl.when((i == 0) & (n_used > 0))
    def _():
        start_gather(0, 0)

    @pl.when(i + 1 < n_used)
    def _():
        start_gather(i + 1, (i + 1) % 2)

    @pl.when(i < n_used)
    def _():
        slot = i % 2
        wait_gather(slot)
        x = xbuf[slot].astype(BF16)
        g = jnp.dot(x, wg_ref[...], preferred_element_type=F32)
        u = jnp.dot(x, wu_ref[...], preferred_element_type=F32)
        h = (g * jax.nn.sigmoid(g) * u).astype(BF16)
        y_ref[...] = jnp.dot(h, wd_ref[...], preferred_element_type=F32) * gate_ref[...]

    @pl.when(i >= n_used)
    def _():
        y_ref[...] = jnp.zeros_like(y_ref)


def moe_experts(xn, row_tok, row_gate, block_expert, n_used, w_gate, w_up, w_down, *, rb):
    d = xn.shape[1]
    n_rows = row_tok.shape[0]
    de = w_gate.shape[2]
    n_blocks = n_rows // rb
    grid_spec = pltpu.PrefetchScalarGridSpec(
        num_scalar_prefetch=3,
        grid=(n_blocks,),
        in_specs=[pl.BlockSpec(memory_space=pl.ANY),
                  pl.BlockSpec((rb, 1), lambda i, be, tok, nu: (i, 0)),
                  pl.BlockSpec((None, d, de), lambda i, be, tok, nu: (be[i], 0, 0)),
                  pl.BlockSpec((None, d, de), lambda i, be, tok, nu: (be[i], 0, 0)),
                  pl.BlockSpec((None, de, d), lambda i, be, tok, nu: (be[i], 0, 0))],
        out_specs=pl.BlockSpec((rb, d), lambda i, be, tok, nu: (i, 0)),
        scratch_shapes=[pltpu.VMEM((2, rb, d), F32), pltpu.SemaphoreType.DMA((2,))],
    )
    return pl.pallas_call(
        functools.partial(_experts_body, rb=rb),
        grid_spec=grid_spec,
        out_shape=jax.ShapeDtypeStruct((n_rows, d), F32),
        compiler_params=_params(1),
        name="moe_experts",
    )(block_expert, row_tok, n_used, xn, row_gate.reshape(n_rows, 1), w_gate, w_up, w_down)


def _combine_body(dest_ref, h_ref, y_hbm, o_ref, buf, sem, *, tc):
    i = pl.program_id(0)

    def start(t, carry):
        for k in range(TOP_K):
            _row_copy(y_hbm, dest_ref[(i * tc + t) * TOP_K + k], buf.at[k], t, sem.at[0]).start()
        return carry

    def wait(t, carry):
        for k in range(TOP_K):
            _row_copy(y_hbm, 0, buf.at[k], t, sem.at[0]).wait()
        return carry

    lax.fori_loop(0, tc, start, 0)
    lax.fori_loop(0, tc, wait, 0)
    o_ref[...] = h_ref[...] + (buf[0] + buf[1])


def moe_combine(h, y_rows, dest, *, tc=128):
    t, d = h.shape
    tc = min(tc, t)
    grid_spec = pltpu.PrefetchScalarGridSpec(
        num_scalar_prefetch=1,
        grid=(t // tc,),
        in_specs=[pl.BlockSpec((tc, d), lambda i, dest: (i, 0)),
                  pl.BlockSpec(memory_space=pl.ANY)],
        out_specs=pl.BlockSpec((tc, d), lambda i, dest: (i, 0)),
        scratch_shapes=[pltpu.VMEM((TOP_K, tc, d), F32), pltpu.SemaphoreType.DMA((1,))],
    )
    return pl.pallas_call(
        functools.partial(_combine_body, tc=tc),
        grid_spec=grid_spec,
        out_shape=jax.ShapeDtypeStruct((t, d), F32),
        compiler_params=_params(1),
        name="moe_combine",
    )(dest, h, y_rows)


def _final_norm_body(x_ref, nw_ref, o_ref):
    o_ref[...] = _rms_scale(x_ref[...], nw_ref[...])


def final_norm(x, nw, *, tm=256):
    m, k = x.shape
    tm = min(tm, m)
    return pl.pallas_call(
        _final_norm_body,
        grid=(m // tm,),
        in_specs=[pl.BlockSpec((tm, k), lambda i: (i, 0)), pl.BlockSpec((1, k), lambda i: (0, 0))],
        out_specs=pl.BlockSpec((tm, k), lambda i: (i, 0)),
        out_shape=jax.ShapeDtypeStruct((m, k), F32),
        compiler_params=_params(1),
        name="final_norm",
    )(x, nw.reshape(1, k))


def _routing_tables(logits, b_group, b_router, rb):
    t = logits.shape[0]
    group_logits = logits[:, :N_GROUPS] + b_group
    group = jnp.argmax(group_logits, axis=-1)
    p_group = jnp.take_along_axis(jax.nn.softmax(group_logits, axis=-1), group[:, None], axis=-1)[:, 0]
    exp_logits = (logits[:, N_GROUPS:N_GROUPS + N_EXPERTS] + b_router).reshape(t, N_GROUPS, EXPERTS_PER_GROUP)
    in_group = jnp.take_along_axis(exp_logits, group[:, None, None], axis=1)[:, 0]
    top_p, top_e = lax.top_k(jax.nn.softmax(in_group, axis=-1), TOP_K)
    gate = (p_group[:, None] * top_p / jnp.sum(top_p, axis=-1, keepdims=True)).reshape(-1)
    eid = (group[:, None] * EXPERTS_PER_GROUP + top_e).reshape(-1).astype(jnp.int32)

    n_assign = t * TOP_K
    onehot = (eid[:, None] == jnp.arange(N_EXPERTS, dtype=jnp.int32)[None, :]).astype(jnp.int32)
    running = jnp.cumsum(onehot, axis=0)
    counts = running[-1]
    rank = jnp.take_along_axis(running, eid[:, None], axis=1)[:, 0] - 1
    padded = (counts + rb - 1) // rb * rb
    padded_end = jnp.cumsum(padded)
    dest = ((padded_end - padded)[eid] + rank).astype(jnp.int32)
    n_rows = n_assign + N_EXPERTS * rb
    n_blocks = n_rows // rb
    row_tok = jnp.zeros((n_rows,), jnp.int32).at[dest].set(jnp.arange(n_assign, dtype=jnp.int32) // TOP_K)
    row_gate = jnp.zeros((n_rows,), F32).at[dest].set(gate)
    block_expert = jnp.minimum(
        jnp.searchsorted(padded_end, jnp.arange(n_blocks, dtype=jnp.int32) * rb, side="right"),
        N_EXPERTS - 1).astype(jnp.int32)
    n_used = (padded_end[-1] // rb).astype(jnp.int32).reshape(1)
    return dest, row_tok, row_gate, block_expert, n_used


def _pad_lanes(a):
    return jnp.pad(a, ((0, 0), (0, LANES - a.shape[1])))


def _hybrid_mixer(h, bsz, seq, norm_w, w_in, conv_w, gla_wf2, gla_bf, gla_norm_w, fox_bf, w_out):
    glr0 = COL_GLA_G + GLA_HEADS * GLA_DV
    ff0 = w_in.shape[1] - FOX_HEADS
    w_main = jnp.concatenate([w_in[:, :glr0], w_in[:, glr0 + GLA_RANK:ff0]], axis=1).astype(BF16)
    w_small = _pad_lanes(jnp.concatenate([w_in[:, glr0:glr0 + GLA_RANK], w_in[:, ff0:]], axis=1)).astype(BF16)
    proj, small = in_proj(h, norm_w, w_main, w_small)

    y_conv = gated_conv(proj, conv_w, bsz, seq)

    wf_pad = jnp.pad(gla_wf2, ((SMALL_LR, LANES - SMALL_LR - GLA_RANK), (0, 0))).astype(BF16)
    y_gla = gated_linear_attention(proj, small, wf_pad, gla_bf.reshape(1, -1), gla_norm_w.reshape(1, -1), bsz, seq)

    bias_row = jnp.pad(fox_bf, (SMALL_F, LANES - SMALL_F - FOX_HEADS)).reshape(1, LANES)
    c = fox_cumulative_gate(small, bias_row, bsz, seq)
    tq = min(512, seq)
    c_blocks = (c[:, SMALL_F:SMALL_F + FOX_HEADS].reshape(bsz, seq, FOX_HEADS)
                .transpose(0, 2, 1).reshape(bsz, FOX_HEADS, seq // tq, tq))
    y_fox = forgetting_attention(proj, c_blocks, bsz, seq, tq=tq)

    return mixer_out_proj(y_conv, y_gla, y_fox, w_out.astype(BF16), h)


def _cross_attention(h, bsz, seq, norm_w, mem2d, mem_norm_w, w_cq, w_ck, w_cv, w_co):
    mem_len = mem2d.shape[0] // bsz
    q = norm_matmul(h, norm_w, w_cq.astype(BF16))
    k = norm_matmul(mem2d, mem_norm_w, w_ck.astype(BF16))
    v = norm_matmul(mem2d, mem_norm_w, w_cv.astype(BF16))
    k_t = k.reshape(bsz, mem_len, D_MODEL).transpose(0, 2, 1)
    return cross_attention_out(q, k_t, v.reshape(bsz, mem_len, D_MODEL), w_co.astype(BF16), h, seq)


def _moe(h, norm_w, w_group, b_group, w_router, b_router, w_gate, w_up, w_down, *, rb=256):
    w_r = _pad_lanes(jnp.concatenate([w_group, w_router], axis=1))
    w_hi = w_r.astype(BF16)
    w_lo = (w_r - w_hi.astype(F32)).astype(BF16)
    xn, logits = moe_router(h, norm_w, w_hi, w_lo)
    dest, row_tok, row_gate, block_expert, n_used = _routing_tables(logits, b_group, b_router, rb)
    y_rows = moe_experts(xn, row_tok, row_gate, block_expert, n_used,
                         w_gate.astype(BF16), w_up.astype(BF16), w_down.astype(BF16), rb=rb)
    return moe_combine(h, y_rows, dest)


def kernel(x, mem, norm_mix_w, w_in, conv_w, gla_wf2, gla_bf, gla_norm_w, fox_bf, w_out, norm_cross_w, mem_norm_w, w_cq, w_ck, w_cv, w_co, norm_ffn_w, w_group, b_group, w_router, b_router, w_expert_gate, w_expert_up, w_expert_down, final_norm_w):
    bsz, seq, d = x.shape
    h = x.reshape(bsz * seq, d)
    mem2d = mem.reshape(-1, d)
    for l in range(norm_mix_w.shape[0]):
        h = _hybrid_mixer(h, bsz, seq, norm_mix_w[l], w_in[l], conv_w[l], gla_wf2[l], gla_bf[l],
                          gla_norm_w[l], fox_bf[l], w_out[l])
        h = _cross_attention(h, bsz, seq, norm_cross_w[l], mem2d, mem_norm_w, w_cq[l], w_ck[l], w_cv[l], w_co[l])
        h = _moe(h, norm_ffn_w[l], w_group[l], b_group[l], w_router[l], b_router[l],
                 w_expert_gate[l], w_expert_up[l], w_expert_down[l])
    return final_norm(h, final_norm_w).reshape(bsz, seq, d)
```

```python
import functools

import jax
import jax.numpy as jnp
from jax import lax
from jax.experimental import pallas as pl
from jax.experimental.pallas import tpu as pltpu

F32 = jnp.float32
BF16 = jnp.bfloat16

D_MODEL = 4096
CONV_WIDTH = 1024
GLA_HEADS = 4
GLA_DK = 128
GLA_DV = 256
GLA_RANK = 16
GLA_TAU = 16.0
GLA_CHUNK = 64
FOX_HEADS = 16
FOX_HD = 128
CROSS_HEADS = 4
CROSS_HD = 1024
N_GROUPS = 4
EXPERTS_PER_GROUP = 8
N_EXPERTS = 32
TOP_K = 2
D_EXPERT = 512
RMS_EPS = 1e-6
LOG2_E = 1.4426950408889634

LANES = 128
BF16_SUBLANES = 16
VMEM_LIMIT = 52 * 1024 * 1024
GATHER_UNROLL = 8

COL_CONV_H, COL_CONV_B, COL_CONV_C = 0, 1024, 2048
COL_GLA_Q, COL_GLA_K, COL_GLA_V, COL_GLA_G = 3072, 3584, 4096, 5120
COL_FOX_Q, COL_FOX_K, COL_FOX_V = 6144, 8192, 10240
MAIN_COLS = 12288
SMALL_LR, SMALL_F = 0, 16


def _params(n_axes):
    return pltpu.CompilerParams(dimension_semantics=("arbitrary",) * n_axes, vmem_limit_bytes=VMEM_LIMIT)


def _log_sigmoid(x):
    return jnp.minimum(x, 0.0) - jnp.log1p(jnp.exp(-jnp.abs(x)))


def _rms_scale(x, nw):
    ms = jnp.mean(x * x, axis=-1, keepdims=True)
    return x * lax.rsqrt(ms + RMS_EPS) * nw


def _norm_matmul_body(x_ref, nw_ref, w_ref, o_ref, xn_ref):
    @pl.when(pl.program_id(1) == 0)
    def _():
        xn_ref[...] = _rms_scale(x_ref[...], nw_ref[...]).astype(BF16)

    o_ref[...] = jnp.dot(xn_ref[...], w_ref[...], preferred_element_type=F32).astype(o_ref.dtype)


def norm_matmul(x, nw, w, *, tm=512, tn=512):
    m, k = x.shape
    n = w.shape[1]
    tm, tn = min(tm, m), min(tn, n)
    return pl.pallas_call(
        _norm_matmul_body,
        grid=(m // tm, n // tn),
        in_specs=[pl.BlockSpec((tm, k), lambda i, j: (i, 0)),
                  pl.BlockSpec((1, k), lambda i, j: (0, 0)),
                  pl.BlockSpec((k, tn), lambda i, j: (0, j))],
        out_specs=pl.BlockSpec((tm, tn), lambda i, j: (i, j)),
        out_shape=jax.ShapeDtypeStruct((m, n), BF16),
        scratch_shapes=[pltpu.VMEM((tm, k), BF16)],
        compiler_params=_params(2),
        name="norm_matmul",
    )(x, nw.reshape(1, k), w)


def _in_proj_body(x_ref, nw_ref, w_ref, ws_ref, o_ref, os_ref, xn_ref):
    @pl.when(pl.program_id(1) == 0)
    def _():
        xn = _rms_scale(x_ref[...], nw_ref[...]).astype(BF16)
        xn_ref[...] = xn
        os_ref[...] = jnp.dot(xn, ws_ref[...], preferred_element_type=F32)

    o_ref[...] = jnp.dot(xn_ref[...], w_ref[...], preferred_element_type=F32).astype(o_ref.dtype)


def in_proj(x, nw, w_main, w_small, *, tm=512, tn=512):
    m, k = x.shape
    n = w_main.shape[1]
    tm = min(tm, m)
    return pl.pallas_call(
        _in_proj_body,
        grid=(m // tm, n // tn),
        in_specs=[pl.BlockSpec((tm, k), lambda i, j: (i, 0)),
                  pl.BlockSpec((1, k), lambda i, j: (0, 0)),
                  pl.BlockSpec((k, tn), lambda i, j: (0, j)),
                  pl.BlockSpec((k, LANES), lambda i, j: (0, 0))],
        out_specs=[pl.BlockSpec((tm, tn), lambda i, j: (i, j)),
                   pl.BlockSpec((tm, LANES), lambda i, j: (i, 0))],
        out_shape=[jax.ShapeDtypeStruct((m, n), BF16), jax.ShapeDtypeStruct((m, LANES), F32)],
        scratch_shapes=[pltpu.VMEM((tm, k), BF16)],
        compiler_params=_params(2),
        name="in_proj",
    )(x, nw.reshape(1, k), w_main, w_small)


def _conv_body(h_ref, b_ref, c_ref, ph_ref, pc_ref, w_ref, o_ref):
    i = pl.program_id(1)
    u = c_ref[...].astype(F32) * h_ref[...].astype(F32)
    up = pc_ref[...].astype(F32) * ph_ref[...].astype(F32)
    up = jnp.where(i > 0, up, 0.0)
    last, last2 = up[BF16_SUBLANES - 1:BF16_SUBLANES], up[BF16_SUBLANES - 2:BF16_SUBLANES - 1]
    row = lax.broadcasted_iota(jnp.int32, u.shape, 0)
    u1 = jnp.where(row == 0, last, pltpu.roll(u, 1, 0))
    u2 = jnp.where(row == 0, last2, jnp.where(row == 1, last, pltpu.roll(u, 2, 0)))
    w = w_ref[...]
    z = w[0:1] * u2 + w[1:2] * u1 + w[2:3] * u
    o_ref[...] = (b_ref[...].astype(F32) * z).astype(BF16)


def gated_conv(proj, conv_w, bsz, seq, *, ts=512):
    ts = min(ts, seq)
    ns = seq // ts
    cw = CONV_WIDTH
    cb = lambda col: pl.BlockSpec((ts, cw), lambda b, i: (b * ns + i, col // cw))
    pb = lambda col: pl.BlockSpec(
        (BF16_SUBLANES, cw),
        lambda b, i: (jnp.maximum((b * seq + i * ts) // BF16_SUBLANES - 1, 0), col // cw))
    return pl.pallas_call(
        _conv_body,
        grid=(bsz, ns),
        in_specs=[cb(COL_CONV_H), cb(COL_CONV_B), cb(COL_CONV_C), pb(COL_CONV_H), pb(COL_CONV_C),
                  pl.BlockSpec(conv_w.shape, lambda b, i: (0, 0))],
        out_specs=pl.BlockSpec((ts, cw), lambda b, i: (b * ns + i, 0)),
        out_shape=jax.ShapeDtypeStruct((bsz * seq, cw), BF16),
        compiler_params=_params(2),
        name="gated_conv",
    )(proj, proj, proj, proj, proj, conv_w)


def _fox_c_body(x_ref, bias_ref, o_ref, carry_ref):
    @pl.when(pl.program_id(1) == 0)
    def _():
        carry_ref[...] = jnp.zeros_like(carry_ref)

    lf = _log_sigmoid(x_ref[...] + bias_ref[...])
    ts = lf.shape[0]
    row = lax.broadcasted_iota(jnp.int32, lf.shape, 0)
    k = 1
    while k < ts:
        lf = lf + jnp.where(row >= k, pltpu.roll(lf, k, 0), 0.0)
        k *= 2
    c = lf + carry_ref[...]
    o_ref[...] = c
    carry_ref[...] = c[ts - 1:ts, :]


def fox_cumulative_gate(small, bias_row, bsz, seq, *, ts=512):
    ts = min(ts, seq)
    ns = seq // ts
    return pl.pallas_call(
        _fox_c_body,
        grid=(bsz, ns),
        in_specs=[pl.BlockSpec((ts, LANES), lambda b, i: (b * ns + i, 0)),
                  pl.BlockSpec((1, LANES), lambda b, i: (0, 0))],
        out_specs=pl.BlockSpec((ts, LANES), lambda b, i: (b * ns + i, 0)),
        out_shape=jax.ShapeDtypeStruct((bsz * seq, LANES), F32),
        scratch_shapes=[pltpu.VMEM((1, LANES), F32)],
        compiler_params=_params(2),
        name="fox_cumgate",
    )(small, bias_row)


_NT = (((1,), (1,)), ((), ()))
_TN = (((0,), (0,)), ((), ()))


def _gla_body(q_ref, k_ref, v_ref, g_ref, lr_ref, wf_ref, bf_ref, nw_ref, o_ref, st_ref, b_ref):
    @pl.when(pl.program_id(1) == 0)
    def _():
        st_ref[...] = jnp.zeros_like(st_ref)

    z = jnp.dot(lr_ref[...].astype(BF16), wf_ref[...], preferred_element_type=F32) + bf_ref[...]
    la = _log_sigmoid(z) * (1.0 / GLA_TAU)
    rin = lax.broadcasted_iota(jnp.int32, la.shape, 0) & (GLA_CHUNK - 1)
    k = 1
    while k < GLA_CHUNK:
        la = la + jnp.where(rin >= k, pltpu.roll(la, k, 0), 0.0)
        k *= 2
    b_ref[...] = la

    tril = (lax.broadcasted_iota(jnp.int32, (GLA_CHUNK, GLA_CHUNK), 0)
            >= lax.broadcasted_iota(jnp.int32, (GLA_CHUNK, GLA_CHUNK), 1))

    def chunk(c, carry):
        r0 = pl.multiple_of(c * GLA_CHUNK, GLA_CHUNK)
        rows = pl.ds(r0, GLA_CHUNK)
        for h in range(GLA_HEADS):
            ks = slice(h * GLA_DK, (h + 1) * GLA_DK)
            vs = slice(h * GLA_DV, (h + 1) * GLA_DV)
            bh = b_ref[rows, ks]
            bl = bh[GLA_CHUNK - 1:GLA_CHUNK, :]
            qh = q_ref[rows, ks].astype(F32) * (GLA_DK ** -0.5)
            kh = k_ref[rows, ks].astype(F32)
            vh = v_ref[rows, vs]
            q_in = (qh * jnp.exp(bh)).astype(BF16)
            k_in = (kh * jnp.exp(-bh)).astype(BF16)
            k_out = (kh * jnp.exp(bl - bh)).astype(BF16)
            att = lax.dot_general(q_in, k_in, _NT, preferred_element_type=F32)
            att = jnp.where(tril, att, 0.0).astype(BF16)
            st = st_ref[h]
            o = (jnp.dot(att, vh, preferred_element_type=F32)
                 + lax.dot_general(q_in, st.astype(BF16), _NT, preferred_element_type=F32))
            kv = lax.dot_general(vh, k_out, _TN, preferred_element_type=F32)
            st_ref[h] = st * jnp.exp(bl) + kv
            on = _rms_scale(o, nw_ref[:, vs])
            gg = g_ref[rows, vs].astype(F32)
            o_ref[rows, vs] = (on * (gg * jax.nn.sigmoid(gg))).astype(BF16)
        return carry

    lax.fori_loop(0, b_ref.shape[0] // GLA_CHUNK, chunk, 0)


def gated_linear_attention(proj, small, wf_pad, bf_row, nw_row, bsz, seq, *, tg=512):
    tg = min(tg, seq)
    ns = seq // tg
    qk_w, v_w = GLA_HEADS * GLA_DK, GLA_HEADS * GLA_DV
    rb = lambda col, w: pl.BlockSpec((tg, w), lambda b, i: (b * ns + i, col // w))
    full = lambda a: pl.BlockSpec(a.shape, lambda b, i: (0,) * a.ndim)
    return pl.pallas_call(
        _gla_body,
        grid=(bsz, ns),
        in_specs=[rb(COL_GLA_Q, qk_w), rb(COL_GLA_K, qk_w), rb(COL_GLA_V, v_w), rb(COL_GLA_G, v_w),
                  pl.BlockSpec((tg, LANES), lambda b, i: (b * ns + i, 0)),
                  full(wf_pad), full(bf_row), full(nw_row)],
        out_specs=pl.BlockSpec((tg, v_w), lambda b, i: (b * ns + i, 0)),
        out_shape=jax.ShapeDtypeStruct((bsz * seq, v_w), BF16),
        scratch_shapes=[pltpu.VMEM((GLA_HEADS, GLA_DV, GLA_DK), F32), pltpu.VMEM((tg, qk_w), F32)],
        compiler_params=_params(2),
        name="gla",
    )(proj, proj, proj, proj, small, wf_pad, bf_row, nw_row)


def _fox_body(q_ref, k_ref, v_ref, c_ref, o_ref, sa_ref, sb_ref, m_ref, l_ref, acc_ref, *, tq):
    qi = pl.program_id(2)
    q = q_ref[...]
    c0 = c_ref[pl.ds(qi, 1), :][:, 0:1]
    m_ref[...] = jnp.full_like(m_ref, -jnp.inf)
    l_ref[...] = jnp.zeros_like(l_ref)
    acc_ref[...] = jnp.zeros_like(acc_ref)
    n_lane_tiles = tq // LANES

    def key_rows(j):
        return pl.ds(pl.multiple_of(j * tq, tq), tq)

    def scores(j, s_ref):
        s_ref[...] = lax.dot_general(q, k_ref[key_rows(j), :], _NT, preferred_element_type=F32)

    def update(j, s_ref, masked):
        s = s_ref[...] - (c_ref[pl.ds(j, 1), :] - c0) * LOG2_E
        if masked:
            keep = (lax.broadcasted_iota(jnp.int32, s.shape, 1) <= lax.broadcasted_iota(jnp.int32, s.shape, 0))
            s = jnp.where(keep, s, -jnp.inf)
        m_prev = m_ref[...]
        m_new = jnp.maximum(m_prev, jnp.max(s, axis=-1, keepdims=True))
        p = jnp.concatenate([jnp.exp2(s[:, t * LANES:(t + 1) * LANES] - m_new) for t in range(n_lane_tiles)], axis=1)
        alpha = jnp.exp2(m_prev - m_new)
        l_ref[...] = alpha * l_ref[...] + jnp.sum(p, axis=-1, keepdims=True)
        acc_ref[...] = alpha * acc_ref[...] + jnp.dot(p.astype(BF16), v_ref[key_rows(j), :],
                                                      preferred_element_type=F32)
        m_ref[...] = m_new

    scores(0, sa_ref)

    def pair(t, carry):
        j = 2 * t
        scores(j + 1, sb_ref)
        update(j, sa_ref, False)
        scores(j + 2, sa_ref)
        update(j + 1, sb_ref, False)
        return carry

    lax.fori_loop(0, qi // 2, pair, 0)

    @pl.when(qi % 2 == 0)
    def _():
        update(qi, sa_ref, True)

    @pl.when(qi % 2 == 1)
    def _():
        scores(qi, sb_ref)
        update(qi - 1, sa_ref, False)
        update(qi, sb_ref, True)

    o_ref[...] = (acc_ref[...] / l_ref[...]).astype(BF16)


def forgetting_attention(proj, c_blocks, bsz, seq, *, tq=512):
    tq = min(tq, seq)
    nq = seq // tq
    hd = FOX_HD
    return pl.pallas_call(
        functools.partial(_fox_body, tq=tq),
        grid=(bsz, FOX_HEADS, nq),
        in_specs=[pl.BlockSpec((tq, hd), lambda b, h, i: (b * nq + i, COL_FOX_Q // hd + h)),
                  pl.BlockSpec((seq, hd), lambda b, h, i: (b, COL_FOX_K // hd + h)),
                  pl.BlockSpec((seq, hd), lambda b, h, i: (b, COL_FOX_V // hd + h)),
                  pl.BlockSpec((None, None, nq, tq), lambda b, h, i: (b, h, 0, 0))],
        out_specs=pl.BlockSpec((tq, hd), lambda b, h, i: (b * nq + i, h)),
        out_shape=jax.ShapeDtypeStruct((bsz * seq, FOX_HEADS * hd), BF16),
        scratch_shapes=[pltpu.VMEM((tq, tq), F32), pltpu.VMEM((tq, tq), F32),
                        pltpu.VMEM((tq, LANES), F32), pltpu.VMEM((tq, LANES), F32), pltpu.VMEM((tq, hd), F32)],
        compiler_params=_params(3),
        name="fox",
    )(proj, proj, proj, c_blocks)


def _mix_out_body(yc_ref, yg_ref, yf_ref, wc_ref, wg_ref, wf_ref, r_ref, o_ref):
    acc = jnp.dot(yc_ref[...], wc_ref[...], preferred_element_type=F32)
    acc += jnp.dot(yg_ref[...], wg_ref[...], preferred_element_type=F32)
    acc += jnp.dot(yf_ref[...], wf_ref[...], preferred_element_type=F32)
    o_ref[...] = r_ref[...] + acc


def mixer_out_proj(y_conv, y_gla, y_fox, w_out, res, *, tm=512, tn=512):
    m = res.shape[0]
    n = w_out.shape[1]
    tm = min(tm, m)
    kc, kg, kf = y_conv.shape[1], y_gla.shape[1], y_fox.shape[1]
    return pl.pallas_call(
        _mix_out_body,
        grid=(m // tm, n // tn),
        in_specs=[pl.BlockSpec((tm, kc), lambda i, j: (i, 0)),
                  pl.BlockSpec((tm, kg), lambda i, j: (i, 0)),
                  pl.BlockSpec((tm, kf), lambda i, j: (i, 0)),
                  pl.BlockSpec((kc, tn), lambda i, j: (0, j)),
                  pl.BlockSpec((kg, tn), lambda i, j: (kc // kg, j)),
                  pl.BlockSpec((kf, tn), lambda i, j: ((kc + kg) // kf, j)),
                  pl.BlockSpec((tm, tn), lambda i, j: (i, j))],
        out_specs=pl.BlockSpec((tm, tn), lambda i, j: (i, j)),
        out_shape=jax.ShapeDtypeStruct((m, n), F32),
        compiler_params=_params(2),
        name="mixer_out_proj",
    )(y_conv, y_gla, y_fox, w_out, w_out, w_out, res)


def _cross_body(q_ref, kt_ref, v_ref, w_ref, r_ref, o_ref, att_ref):
    @pl.when(pl.program_id(1) == 0)
    def _():
        for h in range(CROSS_HEADS):
            hs = slice(h * CROSS_HD, (h + 1) * CROSS_HD)
            s = jnp.dot(q_ref[:, hs], kt_ref[hs, :], preferred_element_type=F32) * (CROSS_HD ** -0.5)
            p = jnp.exp(s - jnp.max(s, axis=-1, keepdims=True))
            p = p / jnp.sum(p, axis=-1, keepdims=True)
            att_ref[:, hs] = jnp.dot(p.astype(BF16), v_ref[:, hs], preferred_element_type=F32).astype(BF16)

    o_ref[...] = r_ref[...] + jnp.dot(att_ref[...], w_ref[...], preferred_element_type=F32)


def cross_attention_out(q, k_t, v, w_co, res, seq, *, tm=512, tn=512):
    m, d = q.shape
    mem_len = v.shape[1]
    tm = min(tm, seq)
    return pl.pallas_call(
        _cross_body,
        grid=(m // tm, d // tn),
        in_specs=[pl.BlockSpec((tm, d), lambda i, j: (i, 0)),
                  pl.BlockSpec((None, d, mem_len), lambda i, j: ((i * tm) // seq, 0, 0)),
                  pl.BlockSpec((None, mem_len, d), lambda i, j: ((i * tm) // seq, 0, 0)),
                  pl.BlockSpec((d, tn), lambda i, j: (0, j)),
                  pl.BlockSpec((tm, tn), lambda i, j: (i, j))],
        out_specs=pl.BlockSpec((tm, tn), lambda i, j: (i, j)),
        out_shape=jax.ShapeDtypeStruct((m, d), F32),
        scratch_shapes=[pltpu.VMEM((tm, d), BF16)],
        compiler_params=_params(2),
        name="cross_attention_out",
    )(q, k_t, v, w_co, res)


def _router_body(x_ref, nw_ref, whi_ref, wlo_ref, xn_ref, lg_ref):
    xn = _rms_scale(x_ref[...], nw_ref[...])
    xn_ref[...] = xn
    hi = xn.astype(BF16)
    lo = (xn - hi.astype(F32)).astype(BF16)
    lg_ref[...] = (jnp.dot(hi, whi_ref[...], preferred_element_type=F32)
                   + (jnp.dot(hi, wlo_ref[...], preferred_element_type=F32)
                      + jnp.dot(lo, whi_ref[...], preferred_element_type=F32)))


def moe_router(x, nw, w_hi, w_lo, *, tm=256):
    m, k = x.shape
    tm = min(tm, m)
    return pl.pallas_call(
        _router_body,
        grid=(m // tm,),
        in_specs=[pl.BlockSpec((tm, k), lambda i: (i, 0)),
                  pl.BlockSpec((1, k), lambda i: (0, 0)),
                  pl.BlockSpec((k, LANES), lambda i: (0, 0)),
                  pl.BlockSpec((k, LANES), lambda i: (0, 0))],
        out_specs=[pl.BlockSpec((tm, k), lambda i: (i, 0)),
                   pl.BlockSpec((tm, LANES), lambda i: (i, 0))],
        out_shape=[jax.ShapeDtypeStruct((m, k), F32), jax.ShapeDtypeStruct((m, LANES), F32)],
        compiler_params=_params(1),
        name="moe_router",
    )(x, nw.reshape(1, k), w_hi, w_lo)


def _row_copy(src_hbm, src_row, dst, dst_row, sem):
    return pltpu.make_async_copy(src_hbm.at[pl.ds(src_row, 1)], dst.at[pl.ds(dst_row, 1)], sem)


def _experts_body(be_ref, tok_ref, nused_ref, x_hbm, gate_ref, wg_ref, wu_ref, wd_ref, y_ref, xbuf, sem, *, rb):
    i = pl.program_id(0)
    n_used = nused_ref[0]

    def start_gather(blk, slot):
        def body(r, carry):
            _row_copy(x_hbm, tok_ref[blk * rb + r], xbuf.at[slot], r, sem.at[slot]).start()
            return carry
        lax.fori_loop(0, rb, body, 0, unroll=GATHER_UNROLL)

    def wait_gather(slot):
        for r in range(rb):
            _row_copy(x_hbm, 0, xbuf.at[slot], r, sem.at[slot]).wait()

    @pl.when((i == 0) & (n_used > 0))
    def _():
        start_gather(0, 0)

    @pl.when(i + 1 < n_used)
    def _():
        start_gather(i + 1, (i + 1) % 2)

    @pl.when(i < n_used)
    def _():
        slot = i % 2
        wait_gather(slot)
        x = xbuf[slot].astype(BF16)
        g = jnp.dot(x, wg_ref[...], preferred_element_type=F32)
        u = jnp.dot(x, wu_ref[...], preferred_element_type=F32)
        h = (g * jax.nn.sigmoid(g) * u).astype(BF16)
        y_ref[...] = jnp.dot(h, wd_ref[...], preferred_element_type=F32) * gate_ref[...]

    @pl.when(i >= n_used)
    def _():
        y_ref[...] = jnp.zeros_like(y_ref)


def moe_experts(xn, row_tok, row_gate, block_expert, n_used, w_gate, w_up, w_down, *, rb):
    d = xn.shape[1]
    n_rows = row_tok.shape[0]
    de = w_gate.shape[2]
    n_blocks = n_rows // rb
    grid_spec = pltpu.PrefetchScalarGridSpec(
        num_scalar_prefetch=3,
        grid=(n_blocks,),
        in_specs=[pl.BlockSpec(memory_space=pl.ANY),
                  pl.BlockSpec((rb, 1), lambda i, be, tok, nu: (i, 0)),
                  pl.BlockSpec((None, d, de), lambda i, be, tok, nu: (be[i], 0, 0)),
                  pl.BlockSpec((None, d, de), lambda i, be, tok, nu: (be[i], 0, 0)),
                  pl.BlockSpec((None, de, d), lambda i, be, tok, nu: (be[i], 0, 0))],
        out_specs=pl.BlockSpec((rb, d), lambda i, be, tok, nu: (i, 0)),
        scratch_shapes=[pltpu.VMEM((2, rb, d), F32), pltpu.SemaphoreType.DMA((2,))],
    )
    return pl.pallas_call(
        functools.partial(_experts_body, rb=rb),
        grid_spec=grid_spec,
        out_shape=jax.ShapeDtypeStruct((n_rows, d), F32),
        compiler_params=_params(1),
        name="moe_experts",
    )(block_expert, row_tok, n_used, xn, row_gate.reshape(n_rows, 1), w_gate, w_up, w_down)


def _combine_body(dest_ref, h_ref, y_hbm, o_ref, buf, sem, *, tc):
    i = pl.program_id(0)

    def start_gather(tile, slot):
        def body(t, carry):
            for k in range(TOP_K):
                _row_copy(y_hbm, dest_ref[(tile * tc + t) * TOP_K + k], buf.at[slot, k], t, sem.at[slot]).start()
            return carry
        lax.fori_loop(0, tc, body, 0, unroll=GATHER_UNROLL // TOP_K)

    @pl.when(i == 0)
    def _():
        start_gather(0, 0)

    @pl.when(i + 1 < pl.num_programs(0))
    def _():
        start_gather(i + 1, (i + 1) % 2)

    slot = i % 2
    for t in range(tc):
        for k in range(TOP_K):
            _row_copy(y_hbm, 0, buf.at[slot, k], t, sem.at[slot]).wait()
    o_ref[...] = h_ref[...] + (buf[slot, 0] + buf[slot, 1])


def moe_combine(h, y_rows, dest, *, tc=128):
    t, d = h.shape
    tc = min(tc, t)
    grid_spec = pltpu.PrefetchScalarGridSpec(
        num_scalar_prefetch=1,
        grid=(t // tc,),
        in_specs=[pl.BlockSpec((tc, d), lambda i, dest: (i, 0)),
                  pl.BlockSpec(memory_space=pl.ANY)],
        out_specs=pl.BlockSpec((tc, d), lambda i, dest: (i, 0)),
        scratch_shapes=[pltpu.VMEM((2, TOP_K, tc, d), F32), pltpu.SemaphoreType.DMA((2,))],
    )
    return pl.pallas_call(
        functools.partial(_combine_body, tc=tc),
        grid_spec=grid_spec,
        out_shape=jax.ShapeDtypeStruct((t, d), F32),
        compiler_params=_params(1),
        name="moe_combine",
    )(dest, h, y_rows)


def _final_norm_body(x_ref, nw_ref, o_ref):
    o_ref[...] = _rms_scale(x_ref[...], nw_ref[...])


def final_norm(x, nw, *, tm=256):
    m, k = x.shape
    tm = min(tm, m)
    return pl.pallas_call(
        _final_norm_body,
        grid=(m // tm,),
        in_specs=[pl.BlockSpec((tm, k), lambda i: (i, 0)), pl.BlockSpec((1, k), lambda i: (0, 0))],
        out_specs=pl.BlockSpec((tm, k), lambda i: (i, 0)),
        out_shape=jax.ShapeDtypeStruct((m, k), F32),
        compiler_params=_params(1),
        name="final_norm",
    )(x, nw.reshape(1, k))


def _routing_tables(logits, b_group, b_router, rb):
    t = logits.shape[0]
    group_logits = logits[:, :N_GROUPS] + b_group
    group = jnp.argmax(group_logits, axis=-1)
    p_group = jnp.take_along_axis(jax.nn.softmax(group_logits, axis=-1), group[:, None], axis=-1)[:, 0]
    exp_logits = (logits[:, N_GROUPS:N_GROUPS + N_EXPERTS] + b_router).reshape(t, N_GROUPS, EXPERTS_PER_GROUP)
    in_group = jnp.take_along_axis(exp_logits, group[:, None, None], axis=1)[:, 0]
    top_p, top_e = lax.top_k(jax.nn.softmax(in_group, axis=-1), TOP_K)
    gate = (p_group[:, None] * top_p / jnp.sum(top_p, axis=-1, keepdims=True)).reshape(-1)
    eid = (group[:, None] * EXPERTS_PER_GROUP + top_e).reshape(-1).astype(jnp.int32)

    n_assign = t * TOP_K
    onehot = (eid[:, None] == jnp.arange(N_EXPERTS, dtype=jnp.int32)[None, :]).astype(jnp.int32)
    running = jnp.cumsum(onehot, axis=0)
    counts = running[-1]
    rank = jnp.take_along_axis(running, eid[:, None], axis=1)[:, 0] - 1
    padded = (counts + rb - 1) // rb * rb
    padded_end = jnp.cumsum(padded)
    dest = ((padded_end - padded)[eid] + rank).astype(jnp.int32)
    n_rows = n_assign + N_EXPERTS * rb
    n_blocks = n_rows // rb
    row_tok = jnp.zeros((n_rows,), jnp.int32).at[dest].set(jnp.arange(n_assign, dtype=jnp.int32) // TOP_K)
    row_gate = jnp.zeros((n_rows,), F32).at[dest].set(gate)
    block_expert = jnp.minimum(
        jnp.searchsorted(padded_end, jnp.arange(n_blocks, dtype=jnp.int32) * rb, side="right"),
        N_EXPERTS - 1).astype(jnp.int32)
    n_used = (padded_end[-1] // rb).astype(jnp.int32).reshape(1)
    return dest, row_tok, row_gate, block_expert, n_used


def _pad_lanes(a):
    return jnp.pad(a, ((0, 0), (0, LANES - a.shape[1])))


def _hybrid_mixer(h, bsz, seq, norm_w, w_in, conv_w, gla_wf2, gla_bf, gla_norm_w, fox_bf, w_out):
    glr0 = COL_GLA_G + GLA_HEADS * GLA_DV
    ff0 = w_in.shape[1] - FOX_HEADS
    fq0 = glr0 + GLA_RANK
    fk0 = fq0 + FOX_HEADS * FOX_HD
    w_fox_q = w_in[:, fq0:fk0] * (FOX_HD ** -0.5 * LOG2_E)
    w_main = jnp.concatenate([w_in[:, :glr0], w_fox_q, w_in[:, fk0:ff0]], axis=1).astype(BF16)
    w_small = _pad_lanes(jnp.concatenate([w_in[:, glr0:glr0 + GLA_RANK], w_in[:, ff0:]], axis=1)).astype(BF16)
    proj, small = in_proj(h, norm_w, w_main, w_small)

    y_conv = gated_conv(proj, conv_w, bsz, seq)

    wf_pad = jnp.pad(gla_wf2, ((SMALL_LR, LANES - SMALL_LR - GLA_RANK), (0, 0))).astype(BF16)
    y_gla = gated_linear_attention(proj, small, wf_pad, gla_bf.reshape(1, -1), gla_norm_w.reshape(1, -1), bsz, seq)

    bias_row = jnp.pad(fox_bf, (SMALL_F, LANES - SMALL_F - FOX_HEADS)).reshape(1, LANES)
    c = fox_cumulative_gate(small, bias_row, bsz, seq)
    tq = min(512, seq)
    c_blocks = (c[:, SMALL_F:SMALL_F + FOX_HEADS].reshape(bsz, seq, FOX_HEADS)
                .transpose(0, 2, 1).reshape(bsz, FOX_HEADS, seq // tq, tq))
    y_fox = forgetting_attention(proj, c_blocks, bsz, seq, tq=tq)

    return mixer_out_proj(y_conv, y_gla, y_fox, w_out.astype(BF16), h)


def _cross_attention(h, bsz, seq, norm_w, mem2d, mem_norm_w, w_cq, w_ck, w_cv, w_co):
    mem_len = mem2d.shape[0] // bsz
    q = norm_matmul(h, norm_w, w_cq.astype(BF16))
    k = norm_matmul(mem2d, mem_norm_w, w_ck.astype(BF16))
    v = norm_matmul(mem2d, mem_norm_w, w_cv.astype(BF16))
    k_t = k.reshape(bsz, mem_len, D_MODEL).transpose(0, 2, 1)
    return cross_attention_out(q, k_t, v.reshape(bsz, mem_len, D_MODEL), w_co.astype(BF16), h, seq)


def _moe(h, norm_w, w_group, b_group, w_router, b_router, w_gate, w_up, w_down, *, rb=256):
    w_r = _pad_lanes(jnp.concatenate([w_group, w_router], axis=1))
    w_hi = w_r.astype(BF16)
    w_lo = (w_r - w_hi.astype(F32)).astype(BF16)
    xn, logits = moe_router(h, norm_w, w_hi, w_lo)
    dest, row_tok, row_gate, block_expert, n_used = _routing_tables(logits, b_group, b_router, rb)
    y_rows = moe_experts(xn, row_tok, row_gate, block_expert, n_used,
                         w_gate.astype(BF16), w_up.astype(BF16), w_down.astype(BF16), rb=rb)
    return moe_combine(h, y_rows, dest)


def kernel(x, mem, norm_mix_w, w_in, conv_w, gla_wf2, gla_bf, gla_norm_w, fox_bf, w_out, norm_cross_w, mem_norm_w, w_cq, w_ck, w_cv, w_co, norm_ffn_w, w_group, b_group, w_router, b_router, w_expert_gate, w_expert_up, w_expert_down, final_norm_w):
    bsz, seq, d = x.shape
    h = x.reshape(bsz * seq, d)
    mem2d = mem.reshape(-1, d)
    for l in range(norm_mix_w.shape[0]):
        h = _hybrid_mixer(h, bsz, seq, norm_mix_w[l], w_in[l], conv_w[l], gla_wf2[l], gla_bf[l],
                          gla_norm_w[l], fox_bf[l], w_out[l])
        h = _cross_attention(h, bsz, seq, norm_cross_w[l], mem2d, mem_norm_w, w_cq[l], w_ck[l], w_cv[l], w_co[l])
        h = _moe(h, norm_ffn_w[l], w_group[l], b_group[l], w_router[l], b_router[l],
                 w_expert_gate[l], w_expert_up[l], w_expert_down[l])
    return final_norm(h, final_norm_w).reshape(bsz, seq, d)
```

```python
import functools

import jax
import jax.numpy as jnp
from jax import lax
from jax.experimental import pallas as pl
from jax.experimental.pallas import tpu as pltpu

F32 = jnp.float32
BF16 = jnp.bfloat16

D_MODEL = 4096
CONV_WIDTH = 1024
GLA_HEADS = 4
GLA_DK = 128
GLA_DV = 256
GLA_RANK = 16
GLA_TAU = 16.0
GLA_CHUNK = 64
FOX_HEADS = 16
FOX_HD = 128
CROSS_HEADS = 4
CROSS_HD = 1024
N_GROUPS = 4
EXPERTS_PER_GROUP = 8
N_EXPERTS = 32
TOP_K = 2
D_EXPERT = 512
RMS_EPS = 1e-6
LOG2_E = 1.4426950408889634
FOX_SKIP_LOG2 = 40.0

LANES = 128
BF16_SUBLANES = 16
VMEM_LIMIT = 52 * 1024 * 1024
GATHER_UNROLL = 8

COL_CONV_H, COL_CONV_B, COL_CONV_C = 0, 1024, 2048
COL_GLA_Q, COL_GLA_K, COL_GLA_V, COL_GLA_G = 3072, 3584, 4096, 5120
COL_FOX_Q, COL_FOX_K, COL_FOX_V = 6144, 8192, 10240
MAIN_COLS = 12288
SMALL_LR, SMALL_F = 0, 16


def _params(n_axes):
    return pltpu.CompilerParams(dimension_semantics=("arbitrary",) * n_axes, vmem_limit_bytes=VMEM_LIMIT)


def _log_sigmoid(x):
    return jnp.minimum(x, 0.0) - jnp.log1p(jnp.exp(-jnp.abs(x)))


def _rms_scale(x, nw):
    ms = jnp.mean(x * x, axis=-1, keepdims=True)
    return x * lax.rsqrt(ms + RMS_EPS) * nw


def _norm_matmul_body(x_ref, nw_ref, w_ref, o_ref, xn_ref):
    @pl.when(pl.program_id(1) == 0)
    def _():
        xn_ref[...] = _rms_scale(x_ref[...], nw_ref[...]).astype(BF16)

    o_ref[...] = jnp.dot(xn_ref[...], w_ref[...], preferred_element_type=F32).astype(o_ref.dtype)


def norm_matmul(x, nw, w, *, tm=512, tn=512):
    m, k = x.shape
    n = w.shape[1]
    tm, tn = min(tm, m), min(tn, n)
    return pl.pallas_call(
        _norm_matmul_body,
        grid=(m // tm, n // tn),
        in_specs=[pl.BlockSpec((tm, k), lambda i, j: (i, 0)),
                  pl.BlockSpec((1, k), lambda i, j: (0, 0)),
                  pl.BlockSpec((k, tn), lambda i, j: (0, j))],
        out_specs=pl.BlockSpec((tm, tn), lambda i, j: (i, j)),
        out_shape=jax.ShapeDtypeStruct((m, n), BF16),
        scratch_shapes=[pltpu.VMEM((tm, k), BF16)],
        compiler_params=_params(2),
        name="norm_matmul",
    )(x, nw.reshape(1, k), w)


def _in_proj_body(x_ref, nw_ref, wa_ref, wb_ref, ws_ref, o_ref, os_ref, xn_ref, *, na):
    j = pl.program_id(1)

    @pl.when(j == 0)
    def _():
        xn = _rms_scale(x_ref[...], nw_ref[...]).astype(BF16)
        xn_ref[...] = xn
        os_ref[...] = jnp.dot(xn, ws_ref[...], preferred_element_type=F32)

    @pl.when(j < na)
    def _():
        o_ref[...] = jnp.dot(xn_ref[...], wa_ref[...], preferred_element_type=F32).astype(o_ref.dtype)

    @pl.when(j >= na)
    def _():
        o_ref[...] = jnp.dot(xn_ref[...], wb_ref[...], preferred_element_type=F32).astype(o_ref.dtype)


def in_proj(x, nw, w_a, w_b, w_small, *, tm=512, tn=512):
    m, k = x.shape
    tm = min(tm, m)
    na, nb = w_a.shape[1] // tn, w_b.shape[1] // tn
    return pl.pallas_call(
        functools.partial(_in_proj_body, na=na),
        grid=(m // tm, na + nb),
        in_specs=[pl.BlockSpec((tm, k), lambda i, j: (i, 0)),
                  pl.BlockSpec((1, k), lambda i, j: (0, 0)),
                  pl.BlockSpec((k, tn), lambda i, j: (0, jnp.minimum(j, na - 1))),
                  pl.BlockSpec((k, tn), lambda i, j: (0, jnp.maximum(j - na, 0))),
                  pl.BlockSpec((k, LANES), lambda i, j: (0, 0))],
        out_specs=[pl.BlockSpec((tm, tn), lambda i, j: (i, j)),
                   pl.BlockSpec((tm, LANES), lambda i, j: (i, 0))],
        out_shape=[jax.ShapeDtypeStruct((m, (na + nb) * tn), BF16), jax.ShapeDtypeStruct((m, LANES), F32)],
        scratch_shapes=[pltpu.VMEM((tm, k), BF16)],
        compiler_params=_params(2),
        name="in_proj",
    )(x, nw.reshape(1, k), w_a, w_b, w_small)


def _conv_body(h_ref, b_ref, c_ref, ph_ref, pc_ref, w_ref, o_ref):
    i = pl.program_id(1)
    u = c_ref[...].astype(F32) * h_ref[...].astype(F32)
    up = pc_ref[...].astype(F32) * ph_ref[...].astype(F32)
    up = jnp.where(i > 0, up, 0.0)
    last, last2 = up[BF16_SUBLANES - 1:BF16_SUBLANES], up[BF16_SUBLANES - 2:BF16_SUBLANES - 1]
    row = lax.broadcasted_iota(jnp.int32, u.shape, 0)
    u1 = jnp.where(row == 0, last, pltpu.roll(u, 1, 0))
    u2 = jnp.where(row == 0, last2, jnp.where(row == 1, last, pltpu.roll(u, 2, 0)))
    w = w_ref[...]
    z = w[0:1] * u2 + w[1:2] * u1 + w[2:3] * u
    o_ref[...] = (b_ref[...].astype(F32) * z).astype(BF16)


def gated_conv(proj, conv_w, bsz, seq, *, ts=512):
    ts = min(ts, seq)
    ns = seq // ts
    cw = CONV_WIDTH
    cb = lambda col: pl.BlockSpec((ts, cw), lambda b, i: (b * ns + i, col // cw))
    pb = lambda col: pl.BlockSpec(
        (BF16_SUBLANES, cw),
        lambda b, i: (jnp.maximum((b * seq + i * ts) // BF16_SUBLANES - 1, 0), col // cw))
    return pl.pallas_call(
        _conv_body,
        grid=(bsz, ns),
        in_specs=[cb(COL_CONV_H), cb(COL_CONV_B), cb(COL_CONV_C), pb(COL_CONV_H), pb(COL_CONV_C),
                  pl.BlockSpec(conv_w.shape, lambda b, i: (0, 0))],
        out_specs=pl.BlockSpec((ts, cw), lambda b, i: (b * ns + i, 0)),
        out_shape=jax.ShapeDtypeStruct((bsz * seq, cw), BF16),
        compiler_params=_params(2),
        name="gated_conv",
    )(proj, proj, proj, proj, proj, conv_w)


def _fox_c_body(x_ref, bias_ref, o_ref, carry_ref):
    @pl.when(pl.program_id(1) == 0)
    def _():
        carry_ref[...] = jnp.zeros_like(carry_ref)

    lf = _log_sigmoid(x_ref[...] + bias_ref[...])
    ts = lf.shape[0]
    row = lax.broadcasted_iota(jnp.int32, lf.shape, 0)
    k = 1
    while k < ts:
        lf = lf + jnp.where(row >= k, pltpu.roll(lf, k, 0), 0.0)
        k *= 2
    c = lf + carry_ref[...]
    o_ref[...] = c
    carry_ref[...] = c[ts - 1:ts, :]


def fox_cumulative_gate(small, bias_row, bsz, seq, *, ts=512):
    ts = min(ts, seq)
    ns = seq // ts
    return pl.pallas_call(
        _fox_c_body,
        grid=(bsz, ns),
        in_specs=[pl.BlockSpec((ts, LANES), lambda b, i: (b * ns + i, 0)),
                  pl.BlockSpec((1, LANES), lambda b, i: (0, 0))],
        out_specs=pl.BlockSpec((ts, LANES), lambda b, i: (b * ns + i, 0)),
        out_shape=jax.ShapeDtypeStruct((bsz * seq, LANES), F32),
        scratch_shapes=[pltpu.VMEM((1, LANES), F32)],
        compiler_params=_params(2),
        name="fox_cumgate",
    )(small, bias_row)


_NT = (((1,), (1,)), ((), ()))
_TN = (((0,), (0,)), ((), ()))


def _gla_body(q_ref, k_ref, v_ref, g_ref, lr_ref, wf_ref, bf_ref, nw_ref, o_ref, st_ref, b_ref):
    @pl.when(pl.program_id(1) == 0)
    def _():
        st_ref[...] = jnp.zeros_like(st_ref)

    z = jnp.dot(lr_ref[...].astype(BF16), wf_ref[...], preferred_element_type=F32) + bf_ref[...]
    la = _log_sigmoid(z) * (1.0 / GLA_TAU)
    rin = lax.broadcasted_iota(jnp.int32, la.shape, 0) & (GLA_CHUNK - 1)
    k = 1
    while k < GLA_CHUNK:
        la = la + jnp.where(rin >= k, pltpu.roll(la, k, 0), 0.0)
        k *= 2
    b_ref[...] = la

    tril = (lax.broadcasted_iota(jnp.int32, (GLA_CHUNK, GLA_CHUNK), 0)
            >= lax.broadcasted_iota(jnp.int32, (GLA_CHUNK, GLA_CHUNK), 1))

    def chunk(c, carry):
        r0 = pl.multiple_of(c * GLA_CHUNK, GLA_CHUNK)
        rows = pl.ds(r0, GLA_CHUNK)
        for h in range(GLA_HEADS):
            ks = slice(h * GLA_DK, (h + 1) * GLA_DK)
            vs = slice(h * GLA_DV, (h + 1) * GLA_DV)
            bh = b_ref[rows, ks]
            bl = bh[GLA_CHUNK - 1:GLA_CHUNK, :]
            qh = q_ref[rows, ks].astype(F32) * (GLA_DK ** -0.5)
            kh = k_ref[rows, ks].astype(F32)
            vh = v_ref[rows, vs]
            q_in = (qh * jnp.exp(bh)).astype(BF16)
            k_in = (kh * jnp.exp(-bh)).astype(BF16)
            k_out = (kh * jnp.exp(bl - bh)).astype(BF16)
            att = lax.dot_general(q_in, k_in, _NT, preferred_element_type=F32)
            att = jnp.where(tril, att, 0.0).astype(BF16)
            st = st_ref[h]
            o = (jnp.dot(att, vh, preferred_element_type=F32)
                 + lax.dot_general(q_in, st.astype(BF16), _NT, preferred_element_type=F32))
            kv = lax.dot_general(vh, k_out, _TN, preferred_element_type=F32)
            st_ref[h] = st * jnp.exp(bl) + kv
            on = _rms_scale(o, nw_ref[:, vs])
            gg = g_ref[rows, vs].astype(F32)
            o_ref[rows, vs] = (on * (gg * jax.nn.sigmoid(gg))).astype(BF16)
        return carry

    lax.fori_loop(0, b_ref.shape[0] // GLA_CHUNK, chunk, 0)


def gated_linear_attention(proj, small, wf_pad, bf_row, nw_row, bsz, seq, *, tg=512):
    tg = min(tg, seq)
    ns = seq // tg
    qk_w, v_w = GLA_HEADS * GLA_DK, GLA_HEADS * GLA_DV
    rb = lambda col, w: pl.BlockSpec((tg, w), lambda b, i: (b * ns + i, col // w))
    full = lambda a: pl.BlockSpec(a.shape, lambda b, i: (0,) * a.ndim)
    return pl.pallas_call(
        _gla_body,
        grid=(bsz, ns),
        in_specs=[rb(COL_GLA_Q, qk_w), rb(COL_GLA_K, qk_w), rb(COL_GLA_V, v_w), rb(COL_GLA_G, v_w),
                  pl.BlockSpec((tg, LANES), lambda b, i: (b * ns + i, 0)),
                  full(wf_pad), full(bf_row), full(nw_row)],
        out_specs=pl.BlockSpec((tg, v_w), lambda b, i: (b * ns + i, 0)),
        out_shape=jax.ShapeDtypeStruct((bsz * seq, v_w), BF16),
        scratch_shapes=[pltpu.VMEM((GLA_HEADS, GLA_DV, GLA_DK), F32), pltpu.VMEM((tg, qk_w), F32)],
        compiler_params=_params(2),
        name="gla",
    )(proj, proj, proj, proj, small, wf_pad, bf_row, nw_row)


def _fox_body(q_ref, k_ref, v_ref, c_ref, bound_ref, o_ref, sa_ref, sb_ref, m_ref, l_ref, acc_ref, *, tq):
    qi = pl.program_id(2)
    q = q_ref[...]
    c0 = c_ref[pl.ds(qi, 1), :][:, 0:1]
    m_ref[...] = jnp.full_like(m_ref, -jnp.inf)
    l_ref[...] = jnp.zeros_like(l_ref)
    acc_ref[...] = jnp.zeros_like(acc_ref)
    n_lane_tiles = tq // LANES

    def key_rows(j):
        return pl.ds(pl.multiple_of(j * tq, tq), tq)

    def scores(j, s_ref):
        s_ref[...] = lax.dot_general(q, k_ref[key_rows(j), :], _NT, preferred_element_type=F32)

    def update(j, s_ref, masked):
        s = s_ref[...] - (c_ref[pl.ds(j, 1), :] - c0) * LOG2_E
        if masked:
            keep = (lax.broadcasted_iota(jnp.int32, s.shape, 1) <= lax.broadcasted_iota(jnp.int32, s.shape, 0))
            s = jnp.where(keep, s, -jnp.inf)
        m_prev = m_ref[...]
        m_new = jnp.maximum(m_prev, jnp.max(s, axis=-1, keepdims=True))
        p = jnp.concatenate([jnp.exp2(s[:, t * LANES:(t + 1) * LANES] - m_new) for t in range(n_lane_tiles)], axis=1)
        alpha = jnp.exp2(m_prev - m_new)
        l_ref[...] = alpha * l_ref[...] + jnp.sum(p, axis=-1, keepdims=True)
        acc_ref[...] = alpha * acc_ref[...] + jnp.dot(p.astype(BF16), v_ref[key_rows(j), :],
                                                      preferred_element_type=F32)
        m_ref[...] = m_new

    scores(qi, sa_ref)
    scores(jnp.maximum(qi - 1, 0), sb_ref)
    update(qi, sa_ref, True)

    m_min = jnp.min(m_ref[...], axis=0, keepdims=True)[:, 0:1]
    lane = lax.broadcasted_iota(jnp.int32, (1, LANES), 1)
    negligible = (bound_ref[pl.ds(qi, 1), :] + FOX_SKIP_LOG2 < m_min) & (lane < qi)
    j0 = jnp.sum(negligible.astype(jnp.int32))
    n_tiles = qi - j0

    def pair(t, carry):
        j = qi - 1 - 2 * t
        scores(jnp.maximum(j - 1, 0), sa_ref)
        update(j, sb_ref, False)
        scores(jnp.maximum(j - 2, 0), sb_ref)
        update(j - 1, sa_ref, False)
        return carry

    lax.fori_loop(0, n_tiles // 2, pair, 0)

    @pl.when(n_tiles % 2 == 1)
    def _():
        update(j0, sb_ref, False)

    o_ref[...] = (acc_ref[...] / l_ref[...]).astype(BF16)


def _fox_norm_body(q_ref, k_ref, qn_ref, kn_ref):
    lane = lax.broadcasted_iota(jnp.int32, (1, LANES), 1)

    def max_row_norms(ref):
        out = jnp.zeros((1, LANES), F32)
        for h in range(FOX_HEADS):
            x = ref[:, h * FOX_HD:(h + 1) * FOX_HD].astype(F32)
            sq = jnp.max(jnp.sum(x * x, axis=-1, keepdims=True), axis=0, keepdims=True)
            out = jnp.where(lane == h, jnp.sqrt(sq), out)
        return out

    qn_ref[...] = max_row_norms(q_ref)
    kn_ref[...] = max_row_norms(k_ref)


def fox_tile_norms(proj, n_tiles, tq):
    w = FOX_HEADS * FOX_HD
    out = jax.ShapeDtypeStruct((n_tiles, 1, LANES), F32)
    return pl.pallas_call(
        _fox_norm_body,
        grid=(n_tiles,),
        in_specs=[pl.BlockSpec((tq, w), lambda i: (i, COL_FOX_Q // w)),
                  pl.BlockSpec((tq, w), lambda i: (i, COL_FOX_K // w))],
        out_specs=[pl.BlockSpec((None, 1, LANES), lambda i: (i, 0, 0))] * 2,
        out_shape=[out, out],
        compiler_params=_params(1),
        name="fox_tile_norms",
    )(proj, proj)


def forgetting_attention(proj, c_blocks, bsz, seq, *, tq=512):
    tq = min(tq, seq)
    nq = seq // tq
    hd = FOX_HD
    qn, kn = fox_tile_norms(proj, bsz * nq, tq)
    per_head = lambda a: a[:, 0, :FOX_HEADS].reshape(bsz, nq, FOX_HEADS).transpose(0, 2, 1)
    kn_run = lax.cummax(per_head(kn), axis=2)
    c_first, c_last = c_blocks[..., 0], c_blocks[..., -1]
    bound = (per_head(qn)[..., :, None] * kn_run[..., None, :]
             - (c_last[..., None, :] - c_first[..., :, None]) * LOG2_E)
    bound = jnp.pad(bound, ((0, 0), (0, 0), (0, 0), (0, LANES - nq)))
    return pl.pallas_call(
        functools.partial(_fox_body, tq=tq),
        grid=(bsz, FOX_HEADS, nq),
        in_specs=[pl.BlockSpec((tq, hd), lambda b, h, i: (b * nq + i, COL_FOX_Q // hd + h)),
                  pl.BlockSpec((seq, hd), lambda b, h, i: (b, COL_FOX_K // hd + h)),
                  pl.BlockSpec((seq, hd), lambda b, h, i: (b, COL_FOX_V // hd + h)),
                  pl.BlockSpec((None, None, nq, tq), lambda b, h, i: (b, h, 0, 0)),
                  pl.BlockSpec((None, None, nq, LANES), lambda b, h, i: (b, h, 0, 0))],
        out_specs=pl.BlockSpec((tq, hd), lambda b, h, i: (b * nq + i, h)),
        out_shape=jax.ShapeDtypeStruct((bsz * seq, FOX_HEADS * hd), BF16),
        scratch_shapes=[pltpu.VMEM((tq, tq), F32), pltpu.VMEM((tq, tq), F32),
                        pltpu.VMEM((tq, LANES), F32), pltpu.VMEM((tq, LANES), F32), pltpu.VMEM((tq, hd), F32)],
        compiler_params=_params(3),
        name="fox",
    )(proj, proj, proj, c_blocks, bound)


def _mix_out_body(yc_ref, yg_ref, yf_ref, wc_ref, wg_ref, wf_ref, r_ref, o_ref):
    acc = jnp.dot(yc_ref[...], wc_ref[...], preferred_element_type=F32)
    acc += jnp.dot(yg_ref[...], wg_ref[...], preferred_element_type=F32)
    acc += jnp.dot(yf_ref[...], wf_ref[...], preferred_element_type=F32)
    o_ref[...] = r_ref[...] + acc


def mixer_out_proj(y_conv, y_gla, y_fox, w_out, res, *, tm=512, tn=512):
    m = res.shape[0]
    n = w_out.shape[1]
    tm = min(tm, m)
    kc, kg, kf = y_conv.shape[1], y_gla.shape[1], y_fox.shape[1]
    return pl.pallas_call(
        _mix_out_body,
        grid=(m // tm, n // tn),
        in_specs=[pl.BlockSpec((tm, kc), lambda i, j: (i, 0)),
                  pl.BlockSpec((tm, kg), lambda i, j: (i, 0)),
                  pl.BlockSpec((tm, kf), lambda i, j: (i, 0)),
                  pl.BlockSpec((kc, tn), lambda i, j: (0, j)),
                  pl.BlockSpec((kg, tn), lambda i, j: (kc // kg, j)),
                  pl.BlockSpec((kf, tn), lambda i, j: ((kc + kg) // kf, j)),
                  pl.BlockSpec((tm, tn), lambda i, j: (i, j))],
        out_specs=pl.BlockSpec((tm, tn), lambda i, j: (i, j)),
        out_shape=jax.ShapeDtypeStruct((m, n), F32),
        compiler_params=_params(2),
        name="mixer_out_proj",
    )(y_conv, y_gla, y_fox, w_out, w_out, w_out, res)


def _cross_body(q_ref, kt_ref, v_ref, w_ref, r_ref, o_ref, att_ref):
    @pl.when(pl.program_id(1) == 0)
    def _():
        for h in range(CROSS_HEADS):
            hs = slice(h * CROSS_HD, (h + 1) * CROSS_HD)
            s = jnp.dot(q_ref[:, hs], kt_ref[hs, :], preferred_element_type=F32) * (CROSS_HD ** -0.5)
            p = jnp.exp(s - jnp.max(s, axis=-1, keepdims=True))
            p = p / jnp.sum(p, axis=-1, keepdims=True)
            att_ref[:, hs] = jnp.dot(p.astype(BF16), v_ref[:, hs], preferred_element_type=F32).astype(BF16)

    o_ref[...] = r_ref[...] + jnp.dot(att_ref[...], w_ref[...], preferred_element_type=F32)


def cross_attention_out(q, k_t, v, w_co, res, seq, *, tm=512, tn=512):
    m, d = q.shape
    mem_len = v.shape[1]
    tm = min(tm, seq)
    return pl.pallas_call(
        _cross_body,
        grid=(m // tm, d // tn),
        in_specs=[pl.BlockSpec((tm, d), lambda i, j: (i, 0)),
                  pl.BlockSpec((None, d, mem_len), lambda i, j: ((i * tm) // seq, 0, 0)),
                  pl.BlockSpec((None, mem_len, d), lambda i, j: ((i * tm) // seq, 0, 0)),
                  pl.BlockSpec((d, tn), lambda i, j: (0, j)),
                  pl.BlockSpec((tm, tn), lambda i, j: (i, j))],
        out_specs=pl.BlockSpec((tm, tn), lambda i, j: (i, j)),
        out_shape=jax.ShapeDtypeStruct((m, d), F32),
        scratch_shapes=[pltpu.VMEM((tm, d), BF16)],
        compiler_params=_params(2),
        name="cross_attention_out",
    )(q, k_t, v, w_co, res)


def _router_body(x_ref, nw_ref, whi_ref, wlo_ref, xn_ref, lg_ref):
    xn = _rms_scale(x_ref[...], nw_ref[...])
    xn_ref[...] = xn
    hi = xn.astype(BF16)
    lo = (xn - hi.astype(F32)).astype(BF16)
    lg_ref[...] = (jnp.dot(hi, whi_ref[...], preferred_element_type=F32)
                   + (jnp.dot(hi, wlo_ref[...], preferred_element_type=F32)
                      + jnp.dot(lo, whi_ref[...], preferred_element_type=F32)))


def moe_router(x, nw, w_hi, w_lo, *, tm=256):
    m, k = x.shape
    tm = min(tm, m)
    return pl.pallas_call(
        _router_body,
        grid=(m // tm,),
        in_specs=[pl.BlockSpec((tm, k), lambda i: (i, 0)),
                  pl.BlockSpec((1, k), lambda i: (0, 0)),
                  pl.BlockSpec((k, LANES), lambda i: (0, 0)),
                  pl.BlockSpec((k, LANES), lambda i: (0, 0))],
        out_specs=[pl.BlockSpec((tm, k), lambda i: (i, 0)),
                   pl.BlockSpec((tm, LANES), lambda i: (i, 0))],
        out_shape=[jax.ShapeDtypeStruct((m, k), F32), jax.ShapeDtypeStruct((m, LANES), F32)],
        compiler_params=_params(1),
        name="moe_router",
    )(x, nw.reshape(1, k), w_hi, w_lo)


def _row_copy(src_hbm, src_row, dst, dst_row, sem):
    return pltpu.make_async_copy(src_hbm.at[pl.ds(src_row, 1)], dst.at[pl.ds(dst_row, 1)], sem)


def _experts_body(be_ref, tok_ref, nused_ref, x_hbm, wg_ref, wu_ref, wd_ref, y_ref, xbuf, sem, *, rb):
    i = pl.program_id(0)
    n_used = nused_ref[0]

    def start_gather(blk, slot):
        def body(r, carry):
            _row_copy(x_hbm, tok_ref[blk * rb + r], xbuf.at[slot], r, sem.at[slot]).start()
            return carry
        lax.fori_loop(0, rb, body, 0, unroll=GATHER_UNROLL)

    def wait_gather(slot):
        for r in range(rb):
            _row_copy(x_hbm, 0, xbuf.at[slot], r, sem.at[slot]).wait()

    @pl.when((i == 0) & (n_used > 0))
    def _():
        start_gather(0, 0)

    @pl.when(i + 1 < n_used)
    def _():
        start_gather(i + 1, (i + 1) % 2)

    @pl.when(i < n_used)
    def _():
        slot = i % 2
        wait_gather(slot)
        x = xbuf[slot].astype(BF16)
        g = jnp.dot(x, wg_ref[...], preferred_element_type=F32)
        u = jnp.dot(x, wu_ref[...], preferred_element_type=F32)
        h = (g * jax.nn.sigmoid(g) * u).astype(BF16)
        y_ref[...] = jnp.dot(h, wd_ref[...], preferred_element_type=F32)

    @pl.when(i >= n_used)
    def _():
        y_ref[...] = jnp.zeros_like(y_ref)


def moe_experts(xn, row_tok, block_expert, n_used, w_gate, w_up, w_down, *, rb):
    d = xn.shape[1]
    n_rows = row_tok.shape[0]
    de = w_gate.shape[2]
    n_blocks = n_rows // rb
    grid_spec = pltpu.PrefetchScalarGridSpec(
        num_scalar_prefetch=3,
        grid=(n_blocks,),
        in_specs=[pl.BlockSpec(memory_space=pl.ANY),
                  pl.BlockSpec((None, d, de), lambda i, be, tok, nu: (be[i], 0, 0)),
                  pl.BlockSpec((None, d, de), lambda i, be, tok, nu: (be[i], 0, 0)),
                  pl.BlockSpec((None, de, d), lambda i, be, tok, nu: (be[i], 0, 0))],
        out_specs=pl.BlockSpec((rb, d), lambda i, be, tok, nu: (i, 0)),
        scratch_shapes=[pltpu.VMEM((2, rb, d), F32), pltpu.SemaphoreType.DMA((2,))],
    )
    return pl.pallas_call(
        functools.partial(_experts_body, rb=rb),
        grid_spec=grid_spec,
        out_shape=jax.ShapeDtypeStruct((n_rows, d), F32),
        compiler_params=_params(1),
        name="moe_experts",
    )(block_expert, row_tok, n_used, xn, w_gate, w_up, w_down)


def _combine_body(dest_ref, h_ref, g_ref, y_hbm, o_ref, buf, sem, *, tc):
    i = pl.program_id(0)

    def start_gather(tile, slot):
        def body(t, carry):
            for k in range(TOP_K):
                _row_copy(y_hbm, dest_ref[(tile * tc + t) * TOP_K + k], buf.at[slot, k], t, sem.at[slot]).start()
            return carry
        lax.fori_loop(0, tc, body, 0, unroll=GATHER_UNROLL // TOP_K)

    @pl.when(i == 0)
    def _():
        start_gather(0, 0)

    @pl.when(i + 1 < pl.num_programs(0))
    def _():
        start_gather(i + 1, (i + 1) % 2)

    slot = i % 2
    for t in range(tc):
        for k in range(TOP_K):
            _row_copy(y_hbm, 0, buf.at[slot, k], t, sem.at[slot]).wait()
    g = g_ref[...]
    o_ref[...] = h_ref[...] + (buf[slot, 0] * g[:, 0:1] + buf[slot, 1] * g[:, 1:2])


def moe_combine(h, y_rows, dest, gates, *, tc=128):
    t, d = h.shape
    tc = min(tc, t)
    grid_spec = pltpu.PrefetchScalarGridSpec(
        num_scalar_prefetch=1,
        grid=(t // tc,),
        in_specs=[pl.BlockSpec((tc, d), lambda i, dest: (i, 0)),
                  pl.BlockSpec((tc, TOP_K), lambda i, dest: (i, 0)),
                  pl.BlockSpec(memory_space=pl.ANY)],
        out_specs=pl.BlockSpec((tc, d), lambda i, dest: (i, 0)),
        scratch_shapes=[pltpu.VMEM((2, TOP_K, tc, d), F32), pltpu.SemaphoreType.DMA((2,))],
    )
    return pl.pallas_call(
        functools.partial(_combine_body, tc=tc),
        grid_spec=grid_spec,
        out_shape=jax.ShapeDtypeStruct((t, d), F32),
        compiler_params=_params(1),
        name="moe_combine",
    )(dest, h, gates, y_rows)


def _final_norm_body(x_ref, nw_ref, o_ref):
    o_ref[...] = _rms_scale(x_ref[...], nw_ref[...])


def final_norm(x, nw, *, tm=256):
    m, k = x.shape
    tm = min(tm, m)
    return pl.pallas_call(
        _final_norm_body,
        grid=(m // tm,),
        in_specs=[pl.BlockSpec((tm, k), lambda i: (i, 0)), pl.BlockSpec((1, k), lambda i: (0, 0))],
        out_specs=pl.BlockSpec((tm, k), lambda i: (i, 0)),
        out_shape=jax.ShapeDtypeStruct((m, k), F32),
        compiler_params=_params(1),
        name="final_norm",
    )(x, nw.reshape(1, k))


def _routing_tables(logits, b_group, b_router, rb):
    t = logits.shape[0]
    group_logits = logits[:, :N_GROUPS] + b_group
    group = jnp.argmax(group_logits, axis=-1)
    p_group = jnp.take_along_axis(jax.nn.softmax(group_logits, axis=-1), group[:, None], axis=-1)[:, 0]
    exp_logits = (logits[:, N_GROUPS:N_GROUPS + N_EXPERTS] + b_router).reshape(t, N_GROUPS, EXPERTS_PER_GROUP)
    in_group = jnp.take_along_axis(exp_logits, group[:, None, None], axis=1)[:, 0]
    top_p, top_e = lax.top_k(jax.nn.softmax(in_group, axis=-1), TOP_K)
    gate = (p_group[:, None] * top_p / jnp.sum(top_p, axis=-1, keepdims=True)).reshape(-1)
    eid = (group[:, None] * EXPERTS_PER_GROUP + top_e).reshape(-1).astype(jnp.int32)

    n_assign = t * TOP_K
    seg = min(512, n_assign)
    onehot = (eid[:, None] == jnp.arange(N_EXPERTS, dtype=jnp.int32)[None, :]).astype(F32)
    within = jnp.einsum("ij,bjk->bik", jnp.tril(jnp.ones((seg, seg), F32)),
                        onehot.reshape(n_assign // seg, seg, N_EXPERTS), preferred_element_type=F32)
    seg_total = within[:, -1, :]
    seg_start = jnp.cumsum(seg_total, axis=0) - seg_total
    running = (within + seg_start[:, None, :]).reshape(n_assign, N_EXPERTS)
    counts = (seg_start[-1] + seg_total[-1]).astype(jnp.int32)
    rank = jnp.sum(running * onehot, axis=1).astype(jnp.int32) - 1
    padded = (counts + rb - 1) // rb * rb
    padded_end = jnp.cumsum(padded)
    dest = ((padded_end - padded)[eid] + rank).astype(jnp.int32)
    n_rows = n_assign + N_EXPERTS * rb
    n_blocks = n_rows // rb
    row_tok = jnp.zeros((n_rows,), jnp.int32).at[dest].set(jnp.arange(n_assign, dtype=jnp.int32) // TOP_K)
    block_expert = jnp.minimum(
        jnp.searchsorted(padded_end, jnp.arange(n_blocks, dtype=jnp.int32) * rb, side="right"),
        N_EXPERTS - 1).astype(jnp.int32)
    n_used = (padded_end[-1] // rb).astype(jnp.int32).reshape(1)
    return dest, row_tok, gate.reshape(t, TOP_K), block_expert, n_used


def _pad_lanes(a):
    return jnp.pad(a, ((0, 0), (0, LANES - a.shape[1])))


def _hybrid_mixer(h, bsz, seq, norm_w, w_in, conv_w, gla_wf2, gla_bf, gla_norm_w, fox_bf, w_out):
    glr0 = COL_GLA_G + GLA_HEADS * GLA_DV
    ff0 = w_in.shape[1] - FOX_HEADS
    fq0 = glr0 + GLA_RANK
    fk0 = fq0 + FOX_HEADS * FOX_HD
    col_scale = jnp.where(jnp.arange(ff0 - fq0) < fk0 - fq0, FOX_HD ** -0.5 * LOG2_E, 1.0).astype(F32)
    w_a = w_in[:, :glr0].astype(BF16)
    w_b = (w_in[:, fq0:ff0] * col_scale[None, :]).astype(BF16)
    w_small = _pad_lanes(jnp.concatenate([w_in[:, glr0:glr0 + GLA_RANK], w_in[:, ff0:]], axis=1)).astype(BF16)
    proj, small = in_proj(h, norm_w, w_a, w_b, w_small)

    y_conv = gated_conv(proj, conv_w, bsz, seq)

    wf_pad = jnp.pad(gla_wf2, ((SMALL_LR, LANES - SMALL_LR - GLA_RANK), (0, 0))).astype(BF16)
    y_gla = gated_linear_attention(proj, small, wf_pad, gla_bf.reshape(1, -1), gla_norm_w.reshape(1, -1), bsz, seq)

    bias_row = jnp.pad(fox_bf, (SMALL_F, LANES - SMALL_F - FOX_HEADS)).reshape(1, LANES)
    c = fox_cumulative_gate(small, bias_row, bsz, seq)
    tq = min(512, seq)
    c_blocks = (c[:, SMALL_F:SMALL_F + FOX_HEADS].reshape(bsz, seq, FOX_HEADS)
                .transpose(0, 2, 1).reshape(bsz, FOX_HEADS, seq // tq, tq))
    y_fox = forgetting_attention(proj, c_blocks, bsz, seq, tq=tq)

    return mixer_out_proj(y_conv, y_gla, y_fox, w_out.astype(BF16), h)


def _cross_attention(h, bsz, seq, norm_w, mem2d, mem_norm_w, w_cq, w_ck, w_cv, w_co):
    mem_len = mem2d.shape[0] // bsz
    q = norm_matmul(h, norm_w, w_cq.astype(BF16))
    k = norm_matmul(mem2d, mem_norm_w, w_ck.astype(BF16))
    v = norm_matmul(mem2d, mem_norm_w, w_cv.astype(BF16))
    k_t = k.reshape(bsz, mem_len, D_MODEL).transpose(0, 2, 1)
    return cross_attention_out(q, k_t, v.reshape(bsz, mem_len, D_MODEL), w_co.astype(BF16), h, seq)


def _moe(h, norm_w, w_group, b_group, w_router, b_router, w_gate, w_up, w_down, *, rb=256):
    w_r = _pad_lanes(jnp.concatenate([w_group, w_router], axis=1))
    w_hi = w_r.astype(BF16)
    w_lo = (w_r - w_hi.astype(F32)).astype(BF16)
    xn, logits = moe_router(h, norm_w, w_hi, w_lo)
    dest, row_tok, gates, block_expert, n_used = _routing_tables(logits, b_group, b_router, rb)
    y_rows = moe_experts(xn, row_tok, block_expert, n_used,
                         w_gate.astype(BF16), w_up.astype(BF16), w_down.astype(BF16), rb=rb)
    return moe_combine(h, y_rows, dest, gates)


def kernel(x, mem, norm_mix_w, w_in, conv_w, gla_wf2, gla_bf, gla_norm_w, fox_bf, w_out, norm_cross_w, mem_norm_w, w_cq, w_ck, w_cv, w_co, norm_ffn_w, w_group, b_group, w_router, b_router, w_expert_gate, w_expert_up, w_expert_down, final_norm_w):
    bsz, seq, d = x.shape
    h = x.reshape(bsz * seq, d)
    mem2d = mem.reshape(-1, d)
    for l in range(norm_mix_w.shape[0]):
        h = _hybrid_mixer(h, bsz, seq, norm_mix_w[l], w_in[l], conv_w[l], gla_wf2[l], gla_bf[l],
                          gla_norm_w[l], fox_bf[l], w_out[l])
        h = _cross_attention(h, bsz, seq, norm_cross_w[l], mem2d, mem_norm_w, w_cq[l], w_ck[l], w_cv[l], w_co[l])
        h = _moe(h, norm_ffn_w[l], w_group[l], b_group[l], w_router[l], b_router[l],
                 w_expert_gate[l], w_expert_up[l], w_expert_down[l])
    return final_norm(h, final_norm_w).reshape(bsz, seq, d)
```

```python
import functools

import jax
import jax.numpy as jnp
from jax import lax
from jax.experimental import pallas as pl
from jax.experimental.pallas import tpu as pltpu

F32 = jnp.float32
BF16 = jnp.bfloat16

D_MODEL = 4096
CONV_WIDTH = 1024
GLA_HEADS = 4
GLA_DK = 128
GLA_DV = 256
GLA_RANK = 16
GLA_TAU = 16.0
GLA_CHUNK = 64
FOX_HEADS = 16
FOX_HD = 128
CROSS_HEADS = 4
CROSS_HD = 1024
N_GROUPS = 4
EXPERTS_PER_GROUP = 8
N_EXPERTS = 32
TOP_K = 2
D_EXPERT = 512
RMS_EPS = 1e-6
LOG2_E = 1.4426950408889634
FOX_SKIP_LOG2 = 40.0

LANES = 128
BF16_SUBLANES = 16
VMEM_LIMIT = 52 * 1024 * 1024
GATHER_UNROLL = 8

COL_CONV_H, COL_CONV_B, COL_CONV_C = 0, 1024, 2048
COL_GLA_Q, COL_GLA_K, COL_GLA_V, COL_GLA_G = 3072, 3584, 4096, 5120
COL_FOX_Q, COL_FOX_K, COL_FOX_V = 6144, 8192, 10240
MAIN_COLS = 12288
SMALL_LR, SMALL_F = 0, 16


def _params(n_axes):
    return pltpu.CompilerParams(dimension_semantics=("arbitrary",) * n_axes, vmem_limit_bytes=VMEM_LIMIT)


def _log_sigmoid(x):
    return jnp.minimum(x, 0.0) - jnp.log1p(jnp.exp(-jnp.abs(x)))


def _rms_scale(x, nw):
    ms = jnp.mean(x * x, axis=-1, keepdims=True)
    return x * lax.rsqrt(ms + RMS_EPS) * nw


def _norm_matmul_body(x_ref, nw_ref, w_ref, o_ref, xn_ref):
    @pl.when(pl.program_id(1) == 0)
    def _():
        xn_ref[...] = _rms_scale(x_ref[...], nw_ref[...]).astype(BF16)

    o_ref[...] = jnp.dot(xn_ref[...], w_ref[...], preferred_element_type=F32).astype(o_ref.dtype)


def norm_matmul(x, nw, w, layer, *, tm=512, tn=1024):
    m, k = x.shape
    n = w.shape[2]
    tm, tn = min(tm, m), min(tn, n)
    return pl.pallas_call(
        _norm_matmul_body,
        grid=(m // tm, n // tn),
        in_specs=[pl.BlockSpec((tm, k), lambda i, j: (i, 0)),
                  pl.BlockSpec((1, k), lambda i, j: (0, 0)),
                  pl.BlockSpec((None, k, tn), lambda i, j: (layer, 0, j))],
        out_specs=pl.BlockSpec((tm, tn), lambda i, j: (i, j)),
        out_shape=jax.ShapeDtypeStruct((m, n), BF16),
        scratch_shapes=[pltpu.VMEM((tm, k), BF16)],
        compiler_params=_params(2),
        name="norm_matmul",
    )(x, nw.reshape(1, k), w)


def _in_proj_body(x_ref, nw_ref, wa_ref, wb_ref, ws_ref, o_ref, os_ref, xn_ref, *, na, n_fox_q):
    j = pl.program_id(1)

    @pl.when(j == 0)
    def _():
        xn = _rms_scale(x_ref[...], nw_ref[...]).astype(BF16)
        xn_ref[...] = xn
        os_ref[...] = jnp.dot(xn, ws_ref[...], preferred_element_type=F32)

    @pl.when(j < na)
    def _():
        o_ref[...] = jnp.dot(xn_ref[...], wa_ref[...], preferred_element_type=F32).astype(o_ref.dtype)

    @pl.when((j >= na) & (j < na + n_fox_q))
    def _():
        q = jnp.dot(xn_ref[...], wb_ref[...], preferred_element_type=F32) * (FOX_HD ** -0.5 * LOG2_E)
        o_ref[...] = q.astype(o_ref.dtype)

    @pl.when(j >= na + n_fox_q)
    def _():
        o_ref[...] = jnp.dot(xn_ref[...], wb_ref[...], preferred_element_type=F32).astype(o_ref.dtype)


def in_proj(x, nw, w_in, layer, w_b, w_small, *, tm=512, tn=512):
    m, k = x.shape
    tm = min(tm, m)
    na, nb = COL_FOX_Q // tn, w_b.shape[1] // tn
    return pl.pallas_call(
        functools.partial(_in_proj_body, na=na, n_fox_q=FOX_HEADS * FOX_HD // tn),
        grid=(m // tm, na + nb),
        in_specs=[pl.BlockSpec((tm, k), lambda i, j: (i, 0)),
                  pl.BlockSpec((1, k), lambda i, j: (0, 0)),
                  pl.BlockSpec((None, k, tn), lambda i, j: (layer, 0, jnp.minimum(j, na - 1))),
                  pl.BlockSpec((k, tn), lambda i, j: (0, jnp.maximum(j - na, 0))),
                  pl.BlockSpec((k, LANES), lambda i, j: (0, 0))],
        out_specs=[pl.BlockSpec((tm, tn), lambda i, j: (i, j)),
                   pl.BlockSpec((tm, LANES), lambda i, j: (i, 0))],
        out_shape=[jax.ShapeDtypeStruct((m, (na + nb) * tn), BF16), jax.ShapeDtypeStruct((m, LANES), F32)],
        scratch_shapes=[pltpu.VMEM((tm, k), BF16)],
        compiler_params=_params(2),
        name="in_proj",
    )(x, nw.reshape(1, k), w_in, w_b, w_small)


def _conv_body(h_ref, b_ref, c_ref, ph_ref, pc_ref, w_ref, o_ref):
    i = pl.program_id(1)
    u = c_ref[...].astype(F32) * h_ref[...].astype(F32)
    up = pc_ref[...].astype(F32) * ph_ref[...].astype(F32)
    up = jnp.where(i > 0, up, 0.0)
    last, last2 = up[BF16_SUBLANES - 1:BF16_SUBLANES], up[BF16_SUBLANES - 2:BF16_SUBLANES - 1]
    row = lax.broadcasted_iota(jnp.int32, u.shape, 0)
    u1 = jnp.where(row == 0, last, pltpu.roll(u, 1, 0))
    u2 = jnp.where(row == 0, last2, jnp.where(row == 1, last, pltpu.roll(u, 2, 0)))
    w = w_ref[...]
    z = w[0:1] * u2 + w[1:2] * u1 + w[2:3] * u
    o_ref[...] = (b_ref[...].astype(F32) * z).astype(BF16)


def gated_conv(proj, conv_w, bsz, seq, *, ts=512):
    ts = min(ts, seq)
    ns = seq // ts
    cw = CONV_WIDTH
    cb = lambda col: pl.BlockSpec((ts, cw), lambda b, i: (b * ns + i, col // cw))
    pb = lambda col: pl.BlockSpec(
        (BF16_SUBLANES, cw),
        lambda b, i: (jnp.maximum((b * seq + i * ts) // BF16_SUBLANES - 1, 0), col // cw))
    return pl.pallas_call(
        _conv_body,
        grid=(bsz, ns),
        in_specs=[cb(COL_CONV_H), cb(COL_CONV_B), cb(COL_CONV_C), pb(COL_CONV_H), pb(COL_CONV_C),
                  pl.BlockSpec(conv_w.shape, lambda b, i: (0, 0))],
        out_specs=pl.BlockSpec((ts, cw), lambda b, i: (b * ns + i, 0)),
        out_shape=jax.ShapeDtypeStruct((bsz * seq, cw), BF16),
        compiler_params=_params(2),
        name="gated_conv",
    )(proj, proj, proj, proj, proj, conv_w)


def _fox_c_body(x_ref, bias_ref, o_ref, carry_ref):
    @pl.when(pl.program_id(1) == 0)
    def _():
        carry_ref[...] = jnp.zeros_like(carry_ref)

    lf = _log_sigmoid(x_ref[...] + bias_ref[...])
    ts = lf.shape[0]
    row = lax.broadcasted_iota(jnp.int32, lf.shape, 0)
    k = 1
    while k < ts:
        lf = lf + jnp.where(row >= k, pltpu.roll(lf, k, 0), 0.0)
        k *= 2
    c = lf + carry_ref[...]
    o_ref[...] = c
    carry_ref[...] = c[ts - 1:ts, :]


def fox_cumulative_gate(small, bias_row, bsz, seq, *, ts=512):
    ts = min(ts, seq)
    ns = seq // ts
    return pl.pallas_call(
        _fox_c_body,
        grid=(bsz, ns),
        in_specs=[pl.BlockSpec((ts, LANES), lambda b, i: (b * ns + i, 0)),
                  pl.BlockSpec((1, LANES), lambda b, i: (0, 0))],
        out_specs=pl.BlockSpec((ts, LANES), lambda b, i: (b * ns + i, 0)),
        out_shape=jax.ShapeDtypeStruct((bsz * seq, LANES), F32),
        scratch_shapes=[pltpu.VMEM((1, LANES), F32)],
        compiler_params=_params(2),
        name="fox_cumgate",
    )(small, bias_row)


_NT = (((1,), (1,)), ((), ()))
_TN = (((0,), (0,)), ((), ()))


def _gla_body(q_ref, k_ref, v_ref, g_ref, lr_ref, wf_ref, bf_ref, nw_ref, o_ref, st_ref, b_ref):
    @pl.when(pl.program_id(1) == 0)
    def _():
        st_ref[...] = jnp.zeros_like(st_ref)

    z = jnp.dot(lr_ref[...].astype(BF16), wf_ref[...], preferred_element_type=F32) + bf_ref[...]
    la = _log_sigmoid(z) * (1.0 / GLA_TAU)
    rin = lax.broadcasted_iota(jnp.int32, la.shape, 0) & (GLA_CHUNK - 1)
    k = 1
    while k < GLA_CHUNK:
        la = la + jnp.where(rin >= k, pltpu.roll(la, k, 0), 0.0)
        k *= 2
    b_ref[...] = la

    tril = (lax.broadcasted_iota(jnp.int32, (GLA_CHUNK, GLA_CHUNK), 0)
            >= lax.broadcasted_iota(jnp.int32, (GLA_CHUNK, GLA_CHUNK), 1))

    def chunk(c, carry):
        r0 = pl.multiple_of(c * GLA_CHUNK, GLA_CHUNK)
        rows = pl.ds(r0, GLA_CHUNK)
        for h in range(GLA_HEADS):
            ks = slice(h * GLA_DK, (h + 1) * GLA_DK)
            vs = slice(h * GLA_DV, (h + 1) * GLA_DV)
            bh = b_ref[rows, ks]
            bl = bh[GLA_CHUNK - 1:GLA_CHUNK, :]
            qh = q_ref[rows, ks].astype(F32) * (GLA_DK ** -0.5)
            kh = k_ref[rows, ks].astype(F32)
            vh = v_ref[rows, vs]
            q_in = (qh * jnp.exp(bh)).astype(BF16)
            k_in = (kh * jnp.exp(-bh)).astype(BF16)
            k_out = (kh * jnp.exp(bl - bh)).astype(BF16)
            att = lax.dot_general(q_in, k_in, _NT, preferred_element_type=F32)
            att = jnp.where(tril, att, 0.0).astype(BF16)
            st = st_ref[h]
            o = (jnp.dot(att, vh, preferred_element_type=F32)
                 + lax.dot_general(q_in, st.astype(BF16), _NT, preferred_element_type=F32))
            kv = lax.dot_general(vh, k_out, _TN, preferred_element_type=F32)
            st_ref[h] = st * jnp.exp(bl) + kv
            on = _rms_scale(o, nw_ref[:, vs])
            gg = g_ref[rows, vs].astype(F32)
            o_ref[rows, vs] = (on * (gg * jax.nn.sigmoid(gg))).astype(BF16)
        return carry

    lax.fori_loop(0, b_ref.shape[0] // GLA_CHUNK, chunk, 0)


def gated_linear_attention(proj, small, wf_pad, bf_row, nw_row, bsz, seq, *, tg=512):
    tg = min(tg, seq)
    ns = seq // tg
    qk_w, v_w = GLA_HEADS * GLA_DK, GLA_HEADS * GLA_DV
    rb = lambda col, w: pl.BlockSpec((tg, w), lambda b, i: (b * ns + i, col // w))
    full = lambda a: pl.BlockSpec(a.shape, lambda b, i: (0,) * a.ndim)
    return pl.pallas_call(
        _gla_body,
        grid=(bsz, ns),
        in_specs=[rb(COL_GLA_Q, qk_w), rb(COL_GLA_K, qk_w), rb(COL_GLA_V, v_w), rb(COL_GLA_G, v_w),
                  pl.BlockSpec((tg, LANES), lambda b, i: (b * ns + i, 0)),
                  full(wf_pad), full(bf_row), full(nw_row)],
        out_specs=pl.BlockSpec((tg, v_w), lambda b, i: (b * ns + i, 0)),
        out_shape=jax.ShapeDtypeStruct((bsz * seq, v_w), BF16),
        scratch_shapes=[pltpu.VMEM((GLA_HEADS, GLA_DV, GLA_DK), F32), pltpu.VMEM((tg, qk_w), F32)],
        compiler_params=_params(2),
        name="gla",
    )(proj, proj, proj, proj, small, wf_pad, bf_row, nw_row)


def _fox_body(q_ref, k_ref, v_ref, c_ref, bound_ref, o_ref, sa_ref, sb_ref, m_ref, l_ref, acc_ref, *, tq):
    qi = pl.program_id(2)
    q = q_ref[...]
    c0 = c_ref[pl.ds(qi, 1), :][:, 0:1]
    m_ref[...] = jnp.full_like(m_ref, -jnp.inf)
    l_ref[...] = jnp.zeros_like(l_ref)
    acc_ref[...] = jnp.zeros_like(acc_ref)
    n_lane_tiles = tq // LANES

    def key_rows(j):
        return pl.ds(pl.multiple_of(j * tq, tq), tq)

    def scores(j, s_ref):
        s_ref[...] = lax.dot_general(q, k_ref[key_rows(j), :], _NT, preferred_element_type=F32)

    def update(j, s_ref, masked):
        s = s_ref[...] - (c_ref[pl.ds(j, 1), :] - c0) * LOG2_E
        if masked:
            keep = (lax.broadcasted_iota(jnp.int32, s.shape, 1) <= lax.broadcasted_iota(jnp.int32, s.shape, 0))
            s = jnp.where(keep, s, -jnp.inf)
        m_prev = m_ref[...]
        m_new = jnp.maximum(m_prev, jnp.max(s, axis=-1, keepdims=True))
        p = jnp.concatenate([jnp.exp2(s[:, t * LANES:(t + 1) * LANES] - m_new) for t in range(n_lane_tiles)], axis=1)
        alpha = jnp.exp2(m_prev - m_new)
        l_ref[...] = alpha * l_ref[...] + jnp.sum(p, axis=-1, keepdims=True)
        acc_ref[...] = alpha * acc_ref[...] + jnp.dot(p.astype(BF16), v_ref[key_rows(j), :],
                                                      preferred_element_type=F32)
        m_ref[...] = m_new

    scores(qi, sa_ref)
    scores(jnp.maximum(qi - 1, 0), sb_ref)
    update(qi, sa_ref, True)

    m_min = jnp.min(m_ref[...], axis=0, keepdims=True)[:, 0:1]
    lane = lax.broadcasted_iota(jnp.int32, (1, LANES), 1)
    negligible = (bound_ref[pl.ds(qi, 1), :] + FOX_SKIP_LOG2 < m_min) & (lane < qi)
    j0 = jnp.sum(negligible.astype(jnp.int32))
    n_tiles = qi - j0

    def pair(t, carry):
        j = qi - 1 - 2 * t
        scores(jnp.maximum(j - 1, 0), sa_ref)
        update(j, sb_ref, False)
        scores(jnp.maximum(j - 2, 0), sb_ref)
        update(j - 1, sa_ref, False)
        return carry

    lax.fori_loop(0, n_tiles // 2, pair, 0)

    @pl.when(n_tiles % 2 == 1)
    def _():
        update(j0, sb_ref, False)

    o_ref[...] = (acc_ref[...] / l_ref[...]).astype(BF16)


def _fox_norm_body(q_ref, k_ref, qn_ref, kn_ref):
    lane = lax.broadcasted_iota(jnp.int32, (1, LANES), 1)

    def max_row_norms(ref):
        out = jnp.zeros((1, LANES), F32)
        for h in range(FOX_HEADS):
            x = ref[:, h * FOX_HD:(h + 1) * FOX_HD].astype(F32)
            sq = jnp.max(jnp.sum(x * x, axis=-1, keepdims=True), axis=0, keepdims=True)
            out = jnp.where(lane == h, jnp.sqrt(sq), out)
        return out

    qn_ref[...] = max_row_norms(q_ref)
    kn_ref[...] = max_row_norms(k_ref)


def fox_tile_norms(proj, n_tiles, tq):
    w = FOX_HEADS * FOX_HD
    out = jax.ShapeDtypeStruct((n_tiles, 1, LANES), F32)
    return pl.pallas_call(
        _fox_norm_body,
        grid=(n_tiles,),
        in_specs=[pl.BlockSpec((tq, w), lambda i: (i, COL_FOX_Q // w)),
                  pl.BlockSpec((tq, w), lambda i: (i, COL_FOX_K // w))],
        out_specs=[pl.BlockSpec((None, 1, LANES), lambda i: (i, 0, 0))] * 2,
        out_shape=[out, out],
        compiler_params=_params(1),
        name="fox_tile_norms",
    )(proj, proj)


def forgetting_attention(proj, c_blocks, bsz, seq, *, tq=512):
    tq = min(tq, seq)
    nq = seq // tq
    hd = FOX_HD
    qn, kn = fox_tile_norms(proj, bsz * nq, tq)
    per_head = lambda a: a[:, 0, :FOX_HEADS].reshape(bsz, nq, FOX_HEADS).transpose(0, 2, 1)
    kn_run = lax.cummax(per_head(kn), axis=2)
    c_first, c_last = c_blocks[..., 0], c_blocks[..., -1]
    bound = (per_head(qn)[..., :, None] * kn_run[..., None, :]
             - (c_last[..., None, :] - c_first[..., :, None]) * LOG2_E)
    bound = jnp.pad(bound, ((0, 0), (0, 0), (0, 0), (0, LANES - nq)))
    return pl.pallas_call(
        functools.partial(_fox_body, tq=tq),
        grid=(bsz, FOX_HEADS, nq),
        in_specs=[pl.BlockSpec((tq, hd), lambda b, h, i: (b * nq + i, COL_FOX_Q // hd + h)),
                  pl.BlockSpec((seq, hd), lambda b, h, i: (b, COL_FOX_K // hd + h)),
                  pl.BlockSpec((seq, hd), lambda b, h, i: (b, COL_FOX_V // hd + h)),
                  pl.BlockSpec((None, None, nq, tq), lambda b, h, i: (b, h, 0, 0)),
                  pl.BlockSpec((None, None, nq, LANES), lambda b, h, i: (b, h, 0, 0))],
        out_specs=pl.BlockSpec((tq, hd), lambda b, h, i: (b * nq + i, h)),
        out_shape=jax.ShapeDtypeStruct((bsz * seq, FOX_HEADS * hd), BF16),
        scratch_shapes=[pltpu.VMEM((tq, tq), F32), pltpu.VMEM((tq, tq), F32),
                        pltpu.VMEM((tq, LANES), F32), pltpu.VMEM((tq, LANES), F32), pltpu.VMEM((tq, hd), F32)],
        compiler_params=_params(3),
        name="fox",
    )(proj, proj, proj, c_blocks, bound)


def _mix_out_body(yc_ref, yg_ref, yf_ref, wc_ref, wg_ref, wf_ref, r_ref, o_ref):
    acc = jnp.dot(yc_ref[...], wc_ref[...], preferred_element_type=F32)
    acc += jnp.dot(yg_ref[...], wg_ref[...], preferred_element_type=F32)
    acc += jnp.dot(yf_ref[...], wf_ref[...], preferred_element_type=F32)
    o_ref[...] = r_ref[...] + acc


def mixer_out_proj(y_conv, y_gla, y_fox, w_out, layer, res, *, tm=512, tn=1024):
    m = res.shape[0]
    n = w_out.shape[2]
    tm = min(tm, m)
    kc, kg, kf = y_conv.shape[1], y_gla.shape[1], y_fox.shape[1]
    return pl.pallas_call(
        _mix_out_body,
        grid=(m // tm, n // tn),
        in_specs=[pl.BlockSpec((tm, kc), lambda i, j: (i, 0)),
                  pl.BlockSpec((tm, kg), lambda i, j: (i, 0)),
                  pl.BlockSpec((tm, kf), lambda i, j: (i, 0)),
                  pl.BlockSpec((None, kc, tn), lambda i, j: (layer, 0, j)),
                  pl.BlockSpec((None, kg, tn), lambda i, j: (layer, kc // kg, j)),
                  pl.BlockSpec((None, kf, tn), lambda i, j: (layer, (kc + kg) // kf, j)),
                  pl.BlockSpec((tm, tn), lambda i, j: (i, j))],
        out_specs=pl.BlockSpec((tm, tn), lambda i, j: (i, j)),
        out_shape=jax.ShapeDtypeStruct((m, n), F32),
        compiler_params=_params(2),
        name="mixer_out_proj",
    )(y_conv, y_gla, y_fox, w_out, w_out, w_out, res)


def _cross_body(q_ref, kt_ref, v_ref, w_ref, r_ref, o_ref, att_ref):
    @pl.when(pl.program_id(1) == 0)
    def _():
        for h in range(CROSS_HEADS):
            hs = slice(h * CROSS_HD, (h + 1) * CROSS_HD)
            s = jnp.dot(q_ref[:, hs], kt_ref[hs, :], preferred_element_type=F32) * (CROSS_HD ** -0.5)
            p = jnp.exp(s - jnp.max(s, axis=-1, keepdims=True))
            p = p / jnp.sum(p, axis=-1, keepdims=True)
            att_ref[:, hs] = jnp.dot(p.astype(BF16), v_ref[:, hs], preferred_element_type=F32).astype(BF16)

    o_ref[...] = r_ref[...] + jnp.dot(att_ref[...], w_ref[...], preferred_element_type=F32)


def cross_attention_out(q, k_t, v, w_co, layer, res, seq, *, tm=512, tn=1024):
    m, d = q.shape
    mem_len = v.shape[1]
    tm = min(tm, seq)
    return pl.pallas_call(
        _cross_body,
        grid=(m // tm, d // tn),
        in_specs=[pl.BlockSpec((tm, d), lambda i, j: (i, 0)),
                  pl.BlockSpec((None, d, mem_len), lambda i, j: ((i * tm) // seq, 0, 0)),
                  pl.BlockSpec((None, mem_len, d), lambda i, j: ((i * tm) // seq, 0, 0)),
                  pl.BlockSpec((None, d, tn), lambda i, j: (layer, 0, j)),
                  pl.BlockSpec((tm, tn), lambda i, j: (i, j))],
        out_specs=pl.BlockSpec((tm, tn), lambda i, j: (i, j)),
        out_shape=jax.ShapeDtypeStruct((m, d), F32),
        scratch_shapes=[pltpu.VMEM((tm, d), BF16)],
        compiler_params=_params(2),
        name="cross_attention_out",
    )(q, k_t, v, w_co, res)


def _router_body(x_ref, nw_ref, whi_ref, wlo_ref, xn_ref, lg_ref):
    xn = _rms_scale(x_ref[...], nw_ref[...])
    xn_ref[...] = xn
    hi = xn.astype(BF16)
    lo = (xn - hi.astype(F32)).astype(BF16)
    lg_ref[...] = (jnp.dot(hi, whi_ref[...], preferred_element_type=F32)
                   + (jnp.dot(hi, wlo_ref[...], preferred_element_type=F32)
                      + jnp.dot(lo, whi_ref[...], preferred_element_type=F32)))


def moe_router(x, nw, w_hi, w_lo, *, tm=256):
    m, k = x.shape
    tm = min(tm, m)
    return pl.pallas_call(
        _router_body,
        grid=(m // tm,),
        in_specs=[pl.BlockSpec((tm, k), lambda i: (i, 0)),
                  pl.BlockSpec((1, k), lambda i: (0, 0)),
                  pl.BlockSpec((k, LANES), lambda i: (0, 0)),
                  pl.BlockSpec((k, LANES), lambda i: (0, 0))],
        out_specs=[pl.BlockSpec((tm, k), lambda i: (i, 0)),
                   pl.BlockSpec((tm, LANES), lambda i: (i, 0))],
        out_shape=[jax.ShapeDtypeStruct((m, k), F32), jax.ShapeDtypeStruct((m, LANES), F32)],
        compiler_params=_params(1),
        name="moe_router",
    )(x, nw.reshape(1, k), w_hi, w_lo)


def _row_copy(src_hbm, src_row, dst, dst_row, sem):
    return pltpu.make_async_copy(src_hbm.at[pl.ds(src_row, 1)], dst.at[pl.ds(dst_row, 1)], sem)


def _experts_body(be_ref, tok_ref, nused_ref, x_hbm, wg_ref, wu_ref, wd_ref, y_ref, xbuf, sem, *, rb):
    i = pl.program_id(0)
    n_used = nused_ref[0]

    def start_gather(blk, slot):
        def body(r, carry):
            _row_copy(x_hbm, tok_ref[blk * rb + r], xbuf.at[slot], r, sem.at[slot]).start()
            return carry
        lax.fori_loop(0, rb, body, 0, unroll=GATHER_UNROLL)

    def wait_gather(slot):
        for r in range(rb):
            _row_copy(x_hbm, 0, xbuf.at[slot], r, sem.at[slot]).wait()

    @pl.when((i == 0) & (n_used > 0))
    def _():
        start_gather(0, 0)

    @pl.when(i + 1 < n_used)
    def _():
        start_gather(i + 1, (i + 1) % 2)

    @pl.when(i < n_used)
    def _():
        slot = i % 2
        wait_gather(slot)
        x = xbuf[slot].astype(BF16)
        g = jnp.dot(x, wg_ref[...], preferred_element_type=F32)
        u = jnp.dot(x, wu_ref[...], preferred_element_type=F32)
        h = (g * jax.nn.sigmoid(g) * u).astype(BF16)
        y_ref[...] = jnp.dot(h, wd_ref[...], preferred_element_type=F32)

    @pl.when(i >= n_used)
    def _():
        y_ref[...] = jnp.zeros_like(y_ref)


def moe_experts(xn, row_tok, block_expert, n_used, w_gate, w_up, w_down, layer, *, rb):
    d = xn.shape[1]
    n_rows = row_tok.shape[0]
    de = w_gate.shape[3]
    n_blocks = n_rows // rb
    grid_spec = pltpu.PrefetchScalarGridSpec(
        num_scalar_prefetch=3,
        grid=(n_blocks,),
        in_specs=[pl.BlockSpec(memory_space=pl.ANY),
                  pl.BlockSpec((None, None, d, de), lambda i, be, tok, nu: (layer, be[i], 0, 0)),
                  pl.BlockSpec((None, None, d, de), lambda i, be, tok, nu: (layer, be[i], 0, 0)),
                  pl.BlockSpec((None, None, de, d), lambda i, be, tok, nu: (layer, be[i], 0, 0))],
        out_specs=pl.BlockSpec((rb, d), lambda i, be, tok, nu: (i, 0)),
        scratch_shapes=[pltpu.VMEM((2, rb, d), F32), pltpu.SemaphoreType.DMA((2,))],
    )
    return pl.pallas_call(
        functools.partial(_experts_body, rb=rb),
        grid_spec=grid_spec,
        out_shape=jax.ShapeDtypeStruct((n_rows, d), F32),
        compiler_params=_params(1),
        name="moe_experts",
    )(block_expert, row_tok, n_used, xn, w_gate, w_up, w_down)


def _combine_body(dest_ref, h_ref, g_ref, y_hbm, o_ref, buf, sem, *, tc):
    i = pl.program_id(0)

    def start_gather(tile, slot):
        def body(t, carry):
            for k in range(TOP_K):
                _row_copy(y_hbm, dest_ref[(tile * tc + t) * TOP_K + k], buf.at[slot, k], t, sem.at[slot]).start()
            return carry
        lax.fori_loop(0, tc, body, 0, unroll=GATHER_UNROLL // TOP_K)

    @pl.when(i == 0)
    def _():
        start_gather(0, 0)

    @pl.when(i + 1 < pl.num_programs(0))
    def _():
        start_gather(i + 1, (i + 1) % 2)

    slot = i % 2
    for t in range(tc):
        for k in range(TOP_K):
            _row_copy(y_hbm, 0, buf.at[slot, k], t, sem.at[slot]).wait()
    g = g_ref[...]
    o_ref[...] = h_ref[...] + (buf[slot, 0] * g[:, 0:1] + buf[slot, 1] * g[:, 1:2])


def moe_combine(h, y_rows, dest, gates, *, tc=128):
    t, d = h.shape
    tc = min(tc, t)
    grid_spec = pltpu.PrefetchScalarGridSpec(
        num_scalar_prefetch=1,
        grid=(t // tc,),
        in_specs=[pl.BlockSpec((tc, d), lambda i, dest: (i, 0)),
                  pl.BlockSpec((tc, TOP_K), lambda i, dest: (i, 0)),
                  pl.BlockSpec(memory_space=pl.ANY)],
        out_specs=pl.BlockSpec((tc, d), lambda i, dest: (i, 0)),
        scratch_shapes=[pltpu.VMEM((2, TOP_K, tc, d), F32), pltpu.SemaphoreType.DMA((2,))],
    )
    return pl.pallas_call(
        functools.partial(_combine_body, tc=tc),
        grid_spec=grid_spec,
        out_shape=jax.ShapeDtypeStruct((t, d), F32),
        compiler_params=_params(1),
        name="moe_combine",
    )(dest, h, gates, y_rows)


def _final_norm_body(x_ref, nw_ref, o_ref):
    o_ref[...] = _rms_scale(x_ref[...], nw_ref[...])


def final_norm(x, nw, *, tm=256):
    m, k = x.shape
    tm = min(tm, m)
    return pl.pallas_call(
        _final_norm_body,
        grid=(m // tm,),
        in_specs=[pl.BlockSpec((tm, k), lambda i: (i, 0)), pl.BlockSpec((1, k), lambda i: (0, 0))],
        out_specs=pl.BlockSpec((tm, k), lambda i: (i, 0)),
        out_shape=jax.ShapeDtypeStruct((m, k), F32),
        compiler_params=_params(1),
        name="final_norm",
    )(x, nw.reshape(1, k))


def _routing_tables(logits, b_group, b_router, rb):
    t = logits.shape[0]
    group_logits = logits[:, :N_GROUPS] + b_group
    group = jnp.argmax(group_logits, axis=-1)
    p_group = jnp.take_along_axis(jax.nn.softmax(group_logits, axis=-1), group[:, None], axis=-1)[:, 0]
    exp_logits = (logits[:, N_GROUPS:N_GROUPS + N_EXPERTS] + b_router).reshape(t, N_GROUPS, EXPERTS_PER_GROUP)
    in_group = jnp.take_along_axis(exp_logits, group[:, None, None], axis=1)[:, 0]
    top_p, top_e = lax.top_k(jax.nn.softmax(in_group, axis=-1), TOP_K)
    gate = (p_group[:, None] * top_p / jnp.sum(top_p, axis=-1, keepdims=True)).reshape(-1)
    eid = (group[:, None] * EXPERTS_PER_GROUP + top_e).reshape(-1).astype(jnp.int32)

    n_assign = t * TOP_K
    seg = min(512, n_assign)
    onehot = (eid[:, None] == jnp.arange(N_EXPERTS, dtype=jnp.int32)[None, :]).astype(F32)
    within = jnp.einsum("ij,bjk->bik", jnp.tril(jnp.ones((seg, seg), F32)),
                        onehot.reshape(n_assign // seg, seg, N_EXPERTS), preferred_element_type=F32)
    seg_total = within[:, -1, :]
    seg_start = jnp.cumsum(seg_total, axis=0) - seg_total
    running = (within + seg_start[:, None, :]).reshape(n_assign, N_EXPERTS)
    counts = (seg_start[-1] + seg_total[-1]).astype(jnp.int32)
    rank = jnp.sum(running * onehot, axis=1).astype(jnp.int32) - 1
    padded = (counts + rb - 1) // rb * rb
    padded_end = jnp.cumsum(padded)
    dest = ((padded_end - padded)[eid] + rank).astype(jnp.int32)
    n_rows = n_assign + N_EXPERTS * rb
    n_blocks = n_rows // rb
    row_tok = jnp.zeros((n_rows,), jnp.int32).at[dest].set(jnp.arange(n_assign, dtype=jnp.int32) // TOP_K)
    block_row0 = jnp.arange(n_blocks, dtype=jnp.int32) * rb
    block_expert = jnp.minimum(jnp.sum((padded_end[None, :] <= block_row0[:, None]).astype(jnp.int32), axis=1),
                               N_EXPERTS - 1)
    n_used = (padded_end[-1] // rb).astype(jnp.int32).reshape(1)
    return dest, row_tok, gate.reshape(t, TOP_K), block_expert, n_used


def _pad_lanes(a):
    return jnp.pad(a, ((0, 0), (0, LANES - a.shape[1])))


def _hybrid_mixer(h, bsz, seq, layer, norm_w, w_in_f32, w_in_bf, conv_w, gla_wf2, gla_bf, gla_norm_w, fox_bf, w_out_bf):
    glr0 = COL_GLA_G + GLA_HEADS * GLA_DV
    ff0 = w_in_f32.shape[2] - FOX_HEADS
    fq0 = glr0 + GLA_RANK
    w_b = w_in_bf[layer, :, fq0:ff0]
    w_small = _pad_lanes(jnp.concatenate([w_in_f32[layer, :, glr0:fq0], w_in_f32[layer, :, ff0:]], axis=1)).astype(BF16)
    proj, small = in_proj(h, norm_w, w_in_bf, layer, w_b, w_small)

    y_conv = gated_conv(proj, conv_w, bsz, seq)

    wf_pad = jnp.pad(gla_wf2, ((SMALL_LR, LANES - SMALL_LR - GLA_RANK), (0, 0))).astype(BF16)
    y_gla = gated_linear_attention(proj, small, wf_pad, gla_bf.reshape(1, -1), gla_norm_w.reshape(1, -1), bsz, seq)

    bias_row = jnp.pad(fox_bf, (SMALL_F, LANES - SMALL_F - FOX_HEADS)).reshape(1, LANES)
    c = fox_cumulative_gate(small, bias_row, bsz, seq)
    tq = min(512, seq)
    c_blocks = (c[:, SMALL_F:SMALL_F + FOX_HEADS].reshape(bsz, seq, FOX_HEADS)
                .transpose(0, 2, 1).reshape(bsz, FOX_HEADS, seq // tq, tq))
    y_fox = forgetting_attention(proj, c_blocks, bsz, seq, tq=tq)

    return mixer_out_proj(y_conv, y_gla, y_fox, w_out_bf, layer, h)


def _cross_attention(h, bsz, seq, layer, norm_w, mem2d, mem_norm_w, w_cq, w_ck, w_cv, w_co):
    mem_len = mem2d.shape[0] // bsz
    q = norm_matmul(h, norm_w, w_cq, layer)
    k = norm_matmul(mem2d, mem_norm_w, w_ck, layer)
    v = norm_matmul(mem2d, mem_norm_w, w_cv, layer)
    k_t = k.reshape(bsz, mem_len, D_MODEL).transpose(0, 2, 1)
    return cross_attention_out(q, k_t, v.reshape(bsz, mem_len, D_MODEL), w_co, layer, h, seq)


def _moe(h, layer, norm_w, w_group, b_group, w_router, b_router, w_gate, w_up, w_down, *, rb=256):
    w_r = _pad_lanes(jnp.concatenate([w_group, w_router], axis=1))
    w_hi = w_r.astype(BF16)
    w_lo = (w_r - w_hi.astype(F32)).astype(BF16)
    xn, logits = moe_router(h, norm_w, w_hi, w_lo)
    dest, row_tok, gates, block_expert, n_used = _routing_tables(logits, b_group, b_router, rb)
    y_rows = moe_experts(xn, row_tok, block_expert, n_used, w_gate, w_up, w_down, layer, rb=rb)
    return moe_combine(h, y_rows, dest, gates)


def kernel(x, mem, norm_mix_w, w_in, conv_w, gla_wf2, gla_bf, gla_norm_w, fox_bf, w_out, norm_cross_w, mem_norm_w, w_cq, w_ck, w_cv, w_co, norm_ffn_w, w_group, b_group, w_router, b_router, w_expert_gate, w_expert_up, w_expert_down, final_norm_w):
    bsz, seq, d = x.shape
    h = x.reshape(bsz * seq, d)
    mem2d = mem.reshape(-1, d)
    w_in_bf, w_out_bf = w_in.astype(BF16), w_out.astype(BF16)
    w_cq_bf, w_ck_bf, w_cv_bf, w_co_bf = (w.astype(BF16) for w in (w_cq, w_ck, w_cv, w_co))
    w_eg_bf, w_eu_bf, w_ed_bf = (w.astype(BF16) for w in (w_expert_gate, w_expert_up, w_expert_down))
    for l in range(norm_mix_w.shape[0]):
        h = _hybrid_mixer(h, bsz, seq, l, norm_mix_w[l], w_in, w_in_bf, conv_w[l], gla_wf2[l], gla_bf[l],
                          gla_norm_w[l], fox_bf[l], w_out_bf)
        h = _cross_attention(h, bsz, seq, l, norm_cross_w[l], mem2d, mem_norm_w, w_cq_bf, w_ck_bf, w_cv_bf, w_co_bf)
        h = _moe(h, l, norm_ffn_w[l], w_group[l], b_group[l], w_router[l], b_router[l], w_eg_bf, w_eu_bf, w_ed_bf)
    return final_norm(h, final_norm_w).reshape(bsz, seq, d)
```

```python
import functools

import jax
import jax.numpy as jnp
from jax import lax
from jax.experimental import pallas as pl
from jax.experimental.pallas import tpu as pltpu

F32 = jnp.float32
BF16 = jnp.bfloat16

D_MODEL = 4096
CONV_WIDTH = 1024
GLA_HEADS = 4
GLA_DK = 128
GLA_DV = 256
GLA_RANK = 16
GLA_TAU = 16.0
GLA_CHUNK = 64
FOX_HEADS = 16
FOX_HD = 128
CROSS_HEADS = 4
CROSS_HD = 1024
N_GROUPS = 4
EXPERTS_PER_GROUP = 8
N_EXPERTS = 32
TOP_K = 2
D_EXPERT = 512
RMS_EPS = 1e-6
LOG2_E = 1.4426950408889634
FOX_SKIP_LOG2 = 40.0

LANES = 128
BF16_SUBLANES = 16
VMEM_LIMIT = 52 * 1024 * 1024
GATHER_UNROLL = 8

COL_CONV_H, COL_CONV_B, COL_CONV_C = 0, 1024, 2048
COL_GLA_Q, COL_GLA_K, COL_GLA_V, COL_GLA_G = 3072, 3584, 4096, 5120
COL_FOX_Q, COL_FOX_K, COL_FOX_V = 6144, 8192, 10240
MAIN_COLS = 12288
SMALL_LR, SMALL_F = 0, 16


def _params(n_axes):
    return pltpu.CompilerParams(dimension_semantics=("arbitrary",) * n_axes, vmem_limit_bytes=VMEM_LIMIT)


def _log_sigmoid(x):
    return jnp.minimum(x, 0.0) - jnp.log1p(jnp.exp(-jnp.abs(x)))


def _rms_scale(x, nw):
    ms = jnp.mean(x * x, axis=-1, keepdims=True)
    return x * lax.rsqrt(ms + RMS_EPS) * nw


def _rms_cast_body(x_ref, nw_ref, o_ref):
    o_ref[...] = _rms_scale(x_ref[...], nw_ref[...]).astype(BF16)


def _rms_cast_small_body(x_ref, nw_ref, ws_ref, o_ref, os_ref):
    xn = _rms_scale(x_ref[...], nw_ref[...]).astype(BF16)
    o_ref[...] = xn
    os_ref[...] = jnp.dot(xn, ws_ref[...], preferred_element_type=F32)


def rms_cast(x, nw, w_small=None, *, tm=256):
    m, k = x.shape
    tm = min(tm, m)
    row = pl.BlockSpec((tm, k), lambda i: (i, 0))
    in_specs = [row, pl.BlockSpec((1, k), lambda i: (0, 0))]
    if w_small is None:
        return pl.pallas_call(
            _rms_cast_body, grid=(m // tm,), in_specs=in_specs, out_specs=row,
            out_shape=jax.ShapeDtypeStruct((m, k), BF16), compiler_params=_params(1), name="rms_cast",
        )(x, nw.reshape(1, k))
    return pl.pallas_call(
        _rms_cast_small_body, grid=(m // tm,),
        in_specs=in_specs + [pl.BlockSpec((k, LANES), lambda i: (0, 0))],
        out_specs=[row, pl.BlockSpec((tm, LANES), lambda i: (i, 0))],
        out_shape=[jax.ShapeDtypeStruct((m, k), BF16), jax.ShapeDtypeStruct((m, LANES), F32)],
        compiler_params=_params(1), name="rms_cast_small",
    )(x, nw.reshape(1, k), w_small)


def _matmul_body(x_ref, w_ref, o_ref):
    o_ref[...] = jnp.dot(x_ref[...], w_ref[...], preferred_element_type=F32).astype(o_ref.dtype)


def matmul(x, w, layer, *, tm=1024, tn=1024):
    m, k = x.shape
    n = w.shape[2]
    tm, tn = min(tm, m), min(tn, n)
    return pl.pallas_call(
        _matmul_body,
        grid=(m // tm, n // tn),
        in_specs=[pl.BlockSpec((tm, k), lambda i, j: (i, 0)),
                  pl.BlockSpec((None, k, tn), lambda i, j: (layer, 0, j))],
        out_specs=pl.BlockSpec((tm, tn), lambda i, j: (i, j)),
        out_shape=jax.ShapeDtypeStruct((m, n), BF16),
        compiler_params=_params(2),
        name="matmul",
    )(x, w)


def _in_proj_body(x_ref, wa_ref, wb_ref, o_ref, *, na, n_fox_q):
    j = pl.program_id(1)

    @pl.when(j < na)
    def _():
        o_ref[...] = jnp.dot(x_ref[...], wa_ref[...], preferred_element_type=F32).astype(o_ref.dtype)

    @pl.when((j >= na) & (j < na + n_fox_q))
    def _():
        q = jnp.dot(x_ref[...], wb_ref[...], preferred_element_type=F32) * (FOX_HD ** -0.5 * LOG2_E)
        o_ref[...] = q.astype(o_ref.dtype)

    @pl.when(j >= na + n_fox_q)
    def _():
        o_ref[...] = jnp.dot(x_ref[...], wb_ref[...], preferred_element_type=F32).astype(o_ref.dtype)


def in_proj(xn, w_in, layer, w_b, *, tm=1024, tn=512):
    m, k = xn.shape
    tm = min(tm, m)
    na, nb = COL_FOX_Q // tn, w_b.shape[1] // tn
    return pl.pallas_call(
        functools.partial(_in_proj_body, na=na, n_fox_q=FOX_HEADS * FOX_HD // tn),
        grid=(m // tm, na + nb),
        in_specs=[pl.BlockSpec((tm, k), lambda i, j: (i, 0)),
                  pl.BlockSpec((None, k, tn), lambda i, j: (layer, 0, jnp.minimum(j, na - 1))),
                  pl.BlockSpec((k, tn), lambda i, j: (0, jnp.maximum(j - na, 0)))],
        out_specs=pl.BlockSpec((tm, tn), lambda i, j: (i, j)),
        out_shape=jax.ShapeDtypeStruct((m, (na + nb) * tn), BF16),
        compiler_params=_params(2),
        name="in_proj",
    )(xn, w_in, w_b)


def _conv_body(h_ref, b_ref, c_ref, ph_ref, pc_ref, w_ref, o_ref):
    i = pl.program_id(1)
    u = c_ref[...].astype(F32) * h_ref[...].astype(F32)
    up = pc_ref[...].astype(F32) * ph_ref[...].astype(F32)
    up = jnp.where(i > 0, up, 0.0)
    last, last2 = up[BF16_SUBLANES - 1:BF16_SUBLANES], up[BF16_SUBLANES - 2:BF16_SUBLANES - 1]
    row = lax.broadcasted_iota(jnp.int32, u.shape, 0)
    u1 = jnp.where(row == 0, last, pltpu.roll(u, 1, 0))
    u2 = jnp.where(row == 0, last2, jnp.where(row == 1, last, pltpu.roll(u, 2, 0)))
    w = w_ref[...]
    z = w[0:1] * u2 + w[1:2] * u1 + w[2:3] * u
    o_ref[...] = (b_ref[...].astype(F32) * z).astype(BF16)


def gated_conv(proj, conv_w, bsz, seq, *, ts=512):
    ts = min(ts, seq)
    ns = seq // ts
    cw = CONV_WIDTH
    cb = lambda col: pl.BlockSpec((ts, cw), lambda b, i: (b * ns + i, col // cw))
    pb = lambda col: pl.BlockSpec(
        (BF16_SUBLANES, cw),
        lambda b, i: (jnp.maximum((b * seq + i * ts) // BF16_SUBLANES - 1, 0), col // cw))
    return pl.pallas_call(
        _conv_body,
        grid=(bsz, ns),
        in_specs=[cb(COL_CONV_H), cb(COL_CONV_B), cb(COL_CONV_C), pb(COL_CONV_H), pb(COL_CONV_C),
                  pl.BlockSpec(conv_w.shape, lambda b, i: (0, 0))],
        out_specs=pl.BlockSpec((ts, cw), lambda b, i: (b * ns + i, 0)),
        out_shape=jax.ShapeDtypeStruct((bsz * seq, cw), BF16),
        compiler_params=_params(2),
        name="gated_conv",
    )(proj, proj, proj, proj, proj, conv_w)


def _fox_c_body(x_ref, bias_ref, o_ref, carry_ref):
    @pl.when(pl.program_id(1) == 0)
    def _():
        carry_ref[...] = jnp.zeros_like(carry_ref)

    lf = _log_sigmoid(x_ref[...] + bias_ref[...])
    ts = lf.shape[0]
    row = lax.broadcasted_iota(jnp.int32, lf.shape, 0)
    k = 1
    while k < ts:
        lf = lf + jnp.where(row >= k, pltpu.roll(lf, k, 0), 0.0)
        k *= 2
    c = lf + carry_ref[...]
    o_ref[...] = c
    carry_ref[...] = c[ts - 1:ts, :]


def fox_cumulative_gate(small, bias_row, bsz, seq, *, ts=512):
    ts = min(ts, seq)
    ns = seq // ts
    return pl.pallas_call(
        _fox_c_body,
        grid=(bsz, ns),
        in_specs=[pl.BlockSpec((ts, LANES), lambda b, i: (b * ns + i, 0)),
                  pl.BlockSpec((1, LANES), lambda b, i: (0, 0))],
        out_specs=pl.BlockSpec((ts, LANES), lambda b, i: (b * ns + i, 0)),
        out_shape=jax.ShapeDtypeStruct((bsz * seq, LANES), F32),
        scratch_shapes=[pltpu.VMEM((1, LANES), F32)],
        compiler_params=_params(2),
        name="fox_cumgate",
    )(small, bias_row)


_NT = (((1,), (1,)), ((), ()))
_TN = (((0,), (0,)), ((), ()))


def _gla_body(q_ref, k_ref, v_ref, g_ref, lr_ref, wf_ref, bf_ref, nw_ref, o_ref, st_ref, b_ref):
    @pl.when(pl.program_id(1) == 0)
    def _():
        st_ref[...] = jnp.zeros_like(st_ref)

    z = jnp.dot(lr_ref[...].astype(BF16), wf_ref[...], preferred_element_type=F32) + bf_ref[...]
    la = _log_sigmoid(z) * (1.0 / GLA_TAU)
    rin = lax.broadcasted_iota(jnp.int32, la.shape, 0) & (GLA_CHUNK - 1)
    k = 1
    while k < GLA_CHUNK:
        la = la + jnp.where(rin >= k, pltpu.roll(la, k, 0), 0.0)
        k *= 2
    b_ref[...] = la

    tril = (lax.broadcasted_iota(jnp.int32, (GLA_CHUNK, GLA_CHUNK), 0)
            >= lax.broadcasted_iota(jnp.int32, (GLA_CHUNK, GLA_CHUNK), 1))

    def chunk(c, carry):
        r0 = pl.multiple_of(c * GLA_CHUNK, GLA_CHUNK)
        rows = pl.ds(r0, GLA_CHUNK)
        for h in range(GLA_HEADS):
            ks = slice(h * GLA_DK, (h + 1) * GLA_DK)
            vs = slice(h * GLA_DV, (h + 1) * GLA_DV)
            bh = b_ref[rows, ks]
            bl = bh[GLA_CHUNK - 1:GLA_CHUNK, :]
            qh = q_ref[rows, ks].astype(F32) * (GLA_DK ** -0.5)
            kh = k_ref[rows, ks].astype(F32)
            vh = v_ref[rows, vs]
            q_in = (qh * jnp.exp(bh)).astype(BF16)
            k_in = (kh * jnp.exp(-bh)).astype(BF16)
            k_out = (kh * jnp.exp(bl - bh)).astype(BF16)
            att = lax.dot_general(q_in, k_in, _NT, preferred_element_type=F32)
            att = jnp.where(tril, att, 0.0).astype(BF16)
            st = st_ref[h]
            o = (jnp.dot(att, vh, preferred_element_type=F32)
                 + lax.dot_general(q_in, st.astype(BF16), _NT, preferred_element_type=F32))
            kv = lax.dot_general(vh, k_out, _TN, preferred_element_type=F32)
            st_ref[h] = st * jnp.exp(bl) + kv
            on = _rms_scale(o, nw_ref[:, vs])
            gg = g_ref[rows, vs].astype(F32)
            o_ref[rows, vs] = (on * (gg * jax.nn.sigmoid(gg))).astype(BF16)
        return carry

    lax.fori_loop(0, b_ref.shape[0] // GLA_CHUNK, chunk, 0)


def gated_linear_attention(proj, small, wf_pad, bf_row, nw_row, bsz, seq, *, tg=512):
    tg = min(tg, seq)
    ns = seq // tg
    qk_w, v_w = GLA_HEADS * GLA_DK, GLA_HEADS * GLA_DV
    rb = lambda col, w: pl.BlockSpec((tg, w), lambda b, i: (b * ns + i, col // w))
    full = lambda a: pl.BlockSpec(a.shape, lambda b, i: (0,) * a.ndim)
    return pl.pallas_call(
        _gla_body,
        grid=(bsz, ns),
        in_specs=[rb(COL_GLA_Q, qk_w), rb(COL_GLA_K, qk_w), rb(COL_GLA_V, v_w), rb(COL_GLA_G, v_w),
                  pl.BlockSpec((tg, LANES), lambda b, i: (b * ns + i, 0)),
                  full(wf_pad), full(bf_row), full(nw_row)],
        out_specs=pl.BlockSpec((tg, v_w), lambda b, i: (b * ns + i, 0)),
        out_shape=jax.ShapeDtypeStruct((bsz * seq, v_w), BF16),
        scratch_shapes=[pltpu.VMEM((GLA_HEADS, GLA_DV, GLA_DK), F32), pltpu.VMEM((tg, qk_w), F32)],
        compiler_params=_params(2),
        name="gla",
    )(proj, proj, proj, proj, small, wf_pad, bf_row, nw_row)


def _fox_body(q_ref, k_ref, v_ref, c_ref, bound_ref, o_ref, sa_ref, sb_ref, m_ref, l_ref, acc_ref, *, tq):
    qi = pl.program_id(2)
    q = q_ref[...]
    c0 = c_ref[pl.ds(qi, 1), :][:, 0:1]
    m_ref[...] = jnp.full_like(m_ref, -jnp.inf)
    l_ref[...] = jnp.zeros_like(l_ref)
    acc_ref[...] = jnp.zeros_like(acc_ref)
    n_lane_tiles = tq // LANES

    def key_rows(j):
        return pl.ds(pl.multiple_of(j * tq, tq), tq)

    def scores(j, s_ref):
        s_ref[...] = lax.dot_general(q, k_ref[key_rows(j), :], _NT, preferred_element_type=F32)

    def update(j, s_ref, masked):
        s = s_ref[...] - (c_ref[pl.ds(j, 1), :] - c0) * LOG2_E
        if masked:
            keep = (lax.broadcasted_iota(jnp.int32, s.shape, 1) <= lax.broadcasted_iota(jnp.int32, s.shape, 0))
            s = jnp.where(keep, s, -jnp.inf)
        m_prev = m_ref[...]
        m_new = jnp.maximum(m_prev, jnp.max(s, axis=-1, keepdims=True))
        p = jnp.concatenate([jnp.exp2(s[:, t * LANES:(t + 1) * LANES] - m_new) for t in range(n_lane_tiles)], axis=1)
        alpha = jnp.exp2(m_prev - m_new)
        l_ref[...] = alpha * l_ref[...] + jnp.sum(p, axis=-1, keepdims=True)
        acc_ref[...] = alpha * acc_ref[...] + jnp.dot(p.astype(BF16), v_ref[key_rows(j), :],
                                                      preferred_element_type=F32)
        m_ref[...] = m_new

    scores(qi, sa_ref)
    scores(jnp.maximum(qi - 1, 0), sb_ref)
    update(qi, sa_ref, True)

    m_min = jnp.min(m_ref[...], axis=0, keepdims=True)[:, 0:1]
    lane = lax.broadcasted_iota(jnp.int32, (1, LANES), 1)
    negligible = (bound_ref[pl.ds(qi, 1), :] + FOX_SKIP_LOG2 < m_min) & (lane < qi)
    j0 = jnp.sum(negligible.astype(jnp.int32))
    n_tiles = qi - j0

    def pair(t, carry):
        j = qi - 1 - 2 * t
        scores(jnp.maximum(j - 1, 0), sa_ref)
        update(j, sb_ref, False)
        scores(jnp.maximum(j - 2, 0), sb_ref)
        update(j - 1, sa_ref, False)
        return carry

    lax.fori_loop(0, n_tiles // 2, pair, 0)

    @pl.when(n_tiles % 2 == 1)
    def _():
        update(j0, sb_ref, False)

    o_ref[...] = (acc_ref[...] / l_ref[...]).astype(BF16)


def _fox_norm_body(q_ref, k_ref, qn_ref, kn_ref):
    lane = lax.broadcasted_iota(jnp.int32, (1, LANES), 1)

    def max_row_norms(ref):
        out = jnp.zeros((1, LANES), F32)
        for h in range(FOX_HEADS):
            x = ref[:, h * FOX_HD:(h + 1) * FOX_HD].astype(F32)
            sq = jnp.max(jnp.sum(x * x, axis=-1, keepdims=True), axis=0, keepdims=True)
            out = jnp.where(lane == h, jnp.sqrt(sq), out)
        return out

    qn_ref[...] = max_row_norms(q_ref)
    kn_ref[...] = max_row_norms(k_ref)


def fox_tile_norms(proj, n_tiles, tq):
    w = FOX_HEADS * FOX_HD
    out = jax.ShapeDtypeStruct((n_tiles, 1, LANES), F32)
    return pl.pallas_call(
        _fox_norm_body,
        grid=(n_tiles,),
        in_specs=[pl.BlockSpec((tq, w), lambda i: (i, COL_FOX_Q // w)),
                  pl.BlockSpec((tq, w), lambda i: (i, COL_FOX_K // w))],
        out_specs=[pl.BlockSpec((None, 1, LANES), lambda i: (i, 0, 0))] * 2,
        out_shape=[out, out],
        compiler_params=_params(1),
        name="fox_tile_norms",
    )(proj, proj)


def forgetting_attention(proj, c_blocks, bsz, seq, *, tq=512):
    tq = min(tq, seq)
    nq = seq // tq
    hd = FOX_HD
    qn, kn = fox_tile_norms(proj, bsz * nq, tq)
    per_head = lambda a: a[:, 0, :FOX_HEADS].reshape(bsz, nq, FOX_HEADS).transpose(0, 2, 1)
    kn_run = lax.cummax(per_head(kn), axis=2)
    c_first, c_last = c_blocks[..., 0], c_blocks[..., -1]
    bound = (per_head(qn)[..., :, None] * kn_run[..., None, :]
             - (c_last[..., None, :] - c_first[..., :, None]) * LOG2_E)
    bound = jnp.pad(bound, ((0, 0), (0, 0), (0, 0), (0, LANES - nq)))
    return pl.pallas_call(
        functools.partial(_fox_body, tq=tq),
        grid=(bsz, FOX_HEADS, nq),
        in_specs=[pl.BlockSpec((tq, hd), lambda b, h, i: (b * nq + i, COL_FOX_Q // hd + h)),
                  pl.BlockSpec((seq, hd), lambda b, h, i: (b, COL_FOX_K // hd + h)),
                  pl.BlockSpec((seq, hd), lambda b, h, i: (b, COL_FOX_V // hd + h)),
                  pl.BlockSpec((None, None, nq, tq), lambda b, h, i: (b, h, 0, 0)),
                  pl.BlockSpec((None, None, nq, LANES), lambda b, h, i: (b, h, 0, 0))],
        out_specs=pl.BlockSpec((tq, hd), lambda b, h, i: (b * nq + i, h)),
        out_shape=jax.ShapeDtypeStruct((bsz * seq, FOX_HEADS * hd), BF16),
        scratch_shapes=[pltpu.VMEM((tq, tq), F32), pltpu.VMEM((tq, tq), F32),
                        pltpu.VMEM((tq, LANES), F32), pltpu.VMEM((tq, LANES), F32), pltpu.VMEM((tq, hd), F32)],
        compiler_params=_params(3),
        name="fox",
    )(proj, proj, proj, c_blocks, bound)


def _mix_out_body(yc_ref, yg_ref, yf_ref, wc_ref, wg_ref, wf_ref, r_ref, o_ref):
    acc = jnp.dot(yc_ref[...], wc_ref[...], preferred_element_type=F32)
    acc += jnp.dot(yg_ref[...], wg_ref[...], preferred_element_type=F32)
    acc += jnp.dot(yf_ref[...], wf_ref[...], preferred_element_type=F32)
    o_ref[...] = r_ref[...] + acc


def mixer_out_proj(y_conv, y_gla, y_fox, w_out, layer, res, *, tm=512, tn=1024):
    m = res.shape[0]
    n = w_out.shape[2]
    tm = min(tm, m)
    kc, kg, kf = y_conv.shape[1], y_gla.shape[1], y_fox.shape[1]
    return pl.pallas_call(
        _mix_out_body,
        grid=(m // tm, n // tn),
        in_specs=[pl.BlockSpec((tm, kc), lambda i, j: (i, 0)),
                  pl.BlockSpec((tm, kg), lambda i, j: (i, 0)),
                  pl.BlockSpec((tm, kf), lambda i, j: (i, 0)),
                  pl.BlockSpec((None, kc, tn), lambda i, j: (layer, 0, j)),
                  pl.BlockSpec((None, kg, tn), lambda i, j: (layer, kc // kg, j)),
                  pl.BlockSpec((None, kf, tn), lambda i, j: (layer, (kc + kg) // kf, j)),
                  pl.BlockSpec((tm, tn), lambda i, j: (i, j))],
        out_specs=pl.BlockSpec((tm, tn), lambda i, j: (i, j)),
        out_shape=jax.ShapeDtypeStruct((m, n), F32),
        compiler_params=_params(2),
        name="mixer_out_proj",
    )(y_conv, y_gla, y_fox, w_out, w_out, w_out, res)


def _cross_body(q_ref, kt_ref, v_ref, w_ref, r_ref, o_ref, att_ref):
    @pl.when(pl.program_id(1) == 0)
    def _():
        for h in range(CROSS_HEADS):
            hs = slice(h * CROSS_HD, (h + 1) * CROSS_HD)
            s = jnp.dot(q_ref[:, hs], kt_ref[hs, :], preferred_element_type=F32) * (CROSS_HD ** -0.5)
            p = jnp.exp(s - jnp.max(s, axis=-1, keepdims=True))
            p = p / jnp.sum(p, axis=-1, keepdims=True)
            att_ref[:, hs] = jnp.dot(p.astype(BF16), v_ref[:, hs], preferred_element_type=F32).astype(BF16)

    o_ref[...] = r_ref[...] + jnp.dot(att_ref[...], w_ref[...], preferred_element_type=F32)


def cross_attention_out(q, k_t, v, w_co, layer, res, seq, *, tm=512, tn=1024):
    m, d = q.shape
    mem_len = v.shape[1]
    tm = min(tm, seq)
    return pl.pallas_call(
        _cross_body,
        grid=(m // tm, d // tn),
        in_specs=[pl.BlockSpec((tm, d), lambda i, j: (i, 0)),
                  pl.BlockSpec((None, d, mem_len), lambda i, j: ((i * tm) // seq, 0, 0)),
                  pl.BlockSpec((None, mem_len, d), lambda i, j: ((i * tm) // seq, 0, 0)),
                  pl.BlockSpec((None, d, tn), lambda i, j: (layer, 0, j)),
                  pl.BlockSpec((tm, tn), lambda i, j: (i, j))],
        out_specs=pl.BlockSpec((tm, tn), lambda i, j: (i, j)),
        out_shape=jax.ShapeDtypeStruct((m, d), F32),
        scratch_shapes=[pltpu.VMEM((tm, d), BF16)],
        compiler_params=_params(2),
        name="cross_attention_out",
    )(q, k_t, v, w_co, res)


def _router_body(x_ref, nw_ref, whi_ref, wlo_ref, xn_ref, lg_ref):
    xn = _rms_scale(x_ref[...], nw_ref[...])
    xn_ref[...] = xn
    hi = xn.astype(BF16)
    lo = (xn - hi.astype(F32)).astype(BF16)
    lg_ref[...] = (jnp.dot(hi, whi_ref[...], preferred_element_type=F32)
                   + (jnp.dot(hi, wlo_ref[...], preferred_element_type=F32)
                      + jnp.dot(lo, whi_ref[...], preferred_element_type=F32)))


def moe_router(x, nw, w_hi, w_lo, *, tm=256):
    m, k = x.shape
    tm = min(tm, m)
    return pl.pallas_call(
        _router_body,
        grid=(m // tm,),
        in_specs=[pl.BlockSpec((tm, k), lambda i: (i, 0)),
                  pl.BlockSpec((1, k), lambda i: (0, 0)),
                  pl.BlockSpec((k, LANES), lambda i: (0, 0)),
                  pl.BlockSpec((k, LANES), lambda i: (0, 0))],
        out_specs=[pl.BlockSpec((tm, k), lambda i: (i, 0)),
                   pl.BlockSpec((tm, LANES), lambda i: (i, 0))],
        out_shape=[jax.ShapeDtypeStruct((m, k), F32), jax.ShapeDtypeStruct((m, LANES), F32)],
        compiler_params=_params(1),
        name="moe_router",
    )(x, nw.reshape(1, k), w_hi, w_lo)


def _row_copy(src_hbm, src_row, dst, dst_row, sem):
    return pltpu.make_async_copy(src_hbm.at[pl.ds(src_row, 1)], dst.at[pl.ds(dst_row, 1)], sem)


def _experts_body(be_ref, tok_ref, nused_ref, x_hbm, wg_ref, wu_ref, wd_ref, y_ref, xbuf, sem, *, rb):
    i = pl.program_id(0)
    n_used = nused_ref[0]

    def start_gather(blk, slot):
        def body(r, carry):
            _row_copy(x_hbm, tok_ref[blk * rb + r], xbuf.at[slot], r, sem.at[slot]).start()
            return carry
        lax.fori_loop(0, rb, body, 0, unroll=GATHER_UNROLL)

    def wait_gather(slot):
        for r in range(rb):
            _row_copy(x_hbm, 0, xbuf.at[slot], r, sem.at[slot]).wait()

    @pl.when((i == 0) & (n_used > 0))
    def _():
        start_gather(0, 0)

    @pl.when(i + 1 < n_used)
    def _():
        start_gather(i + 1, (i + 1) % 2)

    @pl.when(i < n_used)
    def _():
        slot = i % 2
        wait_gather(slot)
        x = xbuf[slot].astype(BF16)
        g = jnp.dot(x, wg_ref[...], preferred_element_type=F32)
        u = jnp.dot(x, wu_ref[...], preferred_element_type=F32)
        h = (g * jax.nn.sigmoid(g) * u).astype(BF16)
        y_ref[...] = jnp.dot(h, wd_ref[...], preferred_element_type=F32)

    @pl.when(i >= n_used)
    def _():
        y_ref[...] = jnp.zeros_like(y_ref)


def moe_experts(xn, row_tok, block_expert, n_used, w_gate, w_up, w_down, layer, *, rb):
    d = xn.shape[1]
    n_rows = row_tok.shape[0]
    de = w_gate.shape[3]
    n_blocks = n_rows // rb
    grid_spec = pltpu.PrefetchScalarGridSpec(
        num_scalar_prefetch=3,
        grid=(n_blocks,),
        in_specs=[pl.BlockSpec(memory_space=pl.ANY),
                  pl.BlockSpec((None, None, d, de), lambda i, be, tok, nu: (layer, be[i], 0, 0)),
                  pl.BlockSpec((None, None, d, de), lambda i, be, tok, nu: (layer, be[i], 0, 0)),
                  pl.BlockSpec((None, None, de, d), lambda i, be, tok, nu: (layer, be[i], 0, 0))],
        out_specs=pl.BlockSpec((rb, d), lambda i, be, tok, nu: (i, 0)),
        scratch_shapes=[pltpu.VMEM((2, rb, d), F32), pltpu.SemaphoreType.DMA((2,))],
    )
    return pl.pallas_call(
        functools.partial(_experts_body, rb=rb),
        grid_spec=grid_spec,
        out_shape=jax.ShapeDtypeStruct((n_rows, d), F32),
        compiler_params=_params(1),
        name="moe_experts",
    )(block_expert, row_tok, n_used, xn, w_gate, w_up, w_down)


def _combine_body(dest_ref, h_ref, g_ref, nw_ref, y_hbm, o_ref, buf, sem, *, tc, normalize):
    i = pl.program_id(0)

    def start_gather(tile, slot):
        def body(t, carry):
            for k in range(TOP_K):
                _row_copy(y_hbm, dest_ref[(tile * tc + t) * TOP_K + k], buf.at[slot, k], t, sem.at[slot]).start()
            return carry
        lax.fori_loop(0, tc, body, 0, unroll=GATHER_UNROLL // TOP_K)

    @pl.when(i == 0)
    def _():
        start_gather(0, 0)

    @pl.when(i + 1 < pl.num_programs(0))
    def _():
        start_gather(i + 1, (i + 1) % 2)

    slot = i % 2
    for t in range(tc):
        for k in range(TOP_K):
            _row_copy(y_hbm, 0, buf.at[slot, k], t, sem.at[slot]).wait()
    g = g_ref[...]
    out = h_ref[...] + (buf[slot, 0] * g[:, 0:1] + buf[slot, 1] * g[:, 1:2])
    o_ref[...] = _rms_scale(out, nw_ref[...]) if normalize else out


def moe_combine(h, y_rows, dest, gates, out_norm_w=None, *, tc=128):
    t, d = h.shape
    tc = min(tc, t)
    normalize = out_norm_w is not None
    nw = (out_norm_w if normalize else jnp.ones((d,), F32)).reshape(1, d)
    grid_spec = pltpu.PrefetchScalarGridSpec(
        num_scalar_prefetch=1,
        grid=(t // tc,),
        in_specs=[pl.BlockSpec((tc, d), lambda i, dest: (i, 0)),
                  pl.BlockSpec((tc, TOP_K), lambda i, dest: (i, 0)),
                  pl.BlockSpec((1, d), lambda i, dest: (0, 0)),
                  pl.BlockSpec(memory_space=pl.ANY)],
        out_specs=pl.BlockSpec((tc, d), lambda i, dest: (i, 0)),
        scratch_shapes=[pltpu.VMEM((2, TOP_K, tc, d), F32), pltpu.SemaphoreType.DMA((2,))],
    )
    return pl.pallas_call(
        functools.partial(_combine_body, tc=tc, normalize=normalize),
        grid_spec=grid_spec,
        out_shape=jax.ShapeDtypeStruct((t, d), F32),
        compiler_params=_params(1),
        name="moe_combine",
    )(dest, h, gates, nw, y_rows)


def _routing_tables(logits, b_group, b_router, rb):
    t = logits.shape[0]
    group_logits = logits[:, :N_GROUPS] + b_group
    group = jnp.argmax(group_logits, axis=-1)
    p_group = jnp.take_along_axis(jax.nn.softmax(group_logits, axis=-1), group[:, None], axis=-1)[:, 0]
    exp_logits = (logits[:, N_GROUPS:N_GROUPS + N_EXPERTS] + b_router).reshape(t, N_GROUPS, EXPERTS_PER_GROUP)
    in_group = jnp.take_along_axis(exp_logits, group[:, None, None], axis=1)[:, 0]
    top_p, top_e = lax.top_k(jax.nn.softmax(in_group, axis=-1), TOP_K)
    gate = (p_group[:, None] * top_p / jnp.sum(top_p, axis=-1, keepdims=True)).reshape(-1)
    eid = (group[:, None] * EXPERTS_PER_GROUP + top_e).reshape(-1).astype(jnp.int32)

    n_assign = t * TOP_K
    seg = min(512, n_assign)
    onehot = (eid[:, None] == jnp.arange(N_EXPERTS, dtype=jnp.int32)[None, :]).astype(F32)
    within = jnp.einsum("ij,bjk->bik", jnp.tril(jnp.ones((seg, seg), F32)),
                        onehot.reshape(n_assign // seg, seg, N_EXPERTS), preferred_element_type=F32)
    seg_total = within[:, -1, :]
    seg_start = jnp.cumsum(seg_total, axis=0) - seg_total
    running = (within + seg_start[:, None, :]).reshape(n_assign, N_EXPERTS)
    counts = (seg_start[-1] + seg_total[-1]).astype(jnp.int32)
    rank = jnp.sum(running * onehot, axis=1).astype(jnp.int32) - 1
    padded = (counts + rb - 1) // rb * rb
    padded_end = jnp.cumsum(padded)
    dest = ((padded_end - padded)[eid] + rank).astype(jnp.int32)
    n_rows = n_assign + N_EXPERTS * rb
    n_blocks = n_rows // rb
    row_tok = jnp.zeros((n_rows,), jnp.int32).at[dest].set(jnp.arange(n_assign, dtype=jnp.int32) // TOP_K)
    block_row0 = jnp.arange(n_blocks, dtype=jnp.int32) * rb
    block_expert = jnp.minimum(jnp.sum((padded_end[None, :] <= block_row0[:, None]).astype(jnp.int32), axis=1),
                               N_EXPERTS - 1)
    n_used = (padded_end[-1] // rb).astype(jnp.int32).reshape(1)
    return dest, row_tok, gate.reshape(t, TOP_K), block_expert, n_used


def _pad_lanes(a):
    return jnp.pad(a, ((0, 0), (0, LANES - a.shape[1])))


def _hybrid_mixer(h, bsz, seq, layer, norm_w, w_in_f32, w_in_bf, conv_w, gla_wf2, gla_bf, gla_norm_w, fox_bf, w_out_bf):
    glr0 = COL_GLA_G + GLA_HEADS * GLA_DV
    ff0 = w_in_f32.shape[2] - FOX_HEADS
    fq0 = glr0 + GLA_RANK
    w_b = w_in_bf[layer, :, fq0:ff0]
    w_small = _pad_lanes(jnp.concatenate([w_in_f32[layer, :, glr0:fq0], w_in_f32[layer, :, ff0:]], axis=1)).astype(BF16)
    xn, small = rms_cast(h, norm_w, w_small)
    proj = in_proj(xn, w_in_bf, layer, w_b)

    y_conv = gated_conv(proj, conv_w, bsz, seq)

    wf_pad = jnp.pad(gla_wf2, ((SMALL_LR, LANES - SMALL_LR - GLA_RANK), (0, 0))).astype(BF16)
    y_gla = gated_linear_attention(proj, small, wf_pad, gla_bf.reshape(1, -1), gla_norm_w.reshape(1, -1), bsz, seq)

    bias_row = jnp.pad(fox_bf, (SMALL_F, LANES - SMALL_F - FOX_HEADS)).reshape(1, LANES)
    c = fox_cumulative_gate(small, bias_row, bsz, seq)
    tq = min(512, seq)
    c_blocks = (c[:, SMALL_F:SMALL_F + FOX_HEADS].reshape(bsz, seq, FOX_HEADS)
                .transpose(0, 2, 1).reshape(bsz, FOX_HEADS, seq // tq, tq))
    y_fox = forgetting_attention(proj, c_blocks, bsz, seq, tq=tq)

    return mixer_out_proj(y_conv, y_gla, y_fox, w_out_bf, layer, h)


def _cross_attention(h, bsz, seq, layer, norm_w, mem_n, w_cq, w_ck, w_cv, w_co):
    mem_len = mem_n.shape[0] // bsz
    q = matmul(rms_cast(h, norm_w), w_cq, layer)
    k = matmul(mem_n, w_ck, layer)
    v = matmul(mem_n, w_cv, layer)
    k_t = k.reshape(bsz, mem_len, D_MODEL).transpose(0, 2, 1)
    return cross_attention_out(q, k_t, v.reshape(bsz, mem_len, D_MODEL), w_co, layer, h, seq)


def _moe(h, layer, norm_w, w_group, b_group, w_router, b_router, w_gate, w_up, w_down, *, out_norm_w=None, rb=256):
    w_r = _pad_lanes(jnp.concatenate([w_group, w_router], axis=1))
    w_hi = w_r.astype(BF16)
    w_lo = (w_r - w_hi.astype(F32)).astype(BF16)
    xn, logits = moe_router(h, norm_w, w_hi, w_lo)
    dest, row_tok, gates, block_expert, n_used = _routing_tables(logits, b_group, b_router, rb)
    y_rows = moe_experts(xn, row_tok, block_expert, n_used, w_gate, w_up, w_down, layer, rb=rb)
    return moe_combine(h, y_rows, dest, gates, out_norm_w)


def kernel(x, mem, norm_mix_w, w_in, conv_w, gla_wf2, gla_bf, gla_norm_w, fox_bf, w_out, norm_cross_w, mem_norm_w, w_cq, w_ck, w_cv, w_co, norm_ffn_w, w_group, b_group, w_router, b_router, w_expert_gate, w_expert_up, w_expert_down, final_norm_w):
    bsz, seq, d = x.shape
    h = x.reshape(bsz * seq, d)
    mem_n = rms_cast(mem.reshape(-1, d), mem_norm_w)
    n_layers = norm_mix_w.shape[0]
    w_in_bf, w_out_bf = w_in.astype(BF16), w_out.astype(BF16)
    w_cq_bf, w_ck_bf, w_cv_bf, w_co_bf = (w.astype(BF16) for w in (w_cq, w_ck, w_cv, w_co))
    w_eg_bf, w_eu_bf, w_ed_bf = (w.astype(BF16) for w in (w_expert_gate, w_expert_up, w_expert_down))
    for l in range(n_layers):
        h = _hybrid_mixer(h, bsz, seq, l, norm_mix_w[l], w_in, w_in_bf, conv_w[l], gla_wf2[l], gla_bf[l],
                          gla_norm_w[l], fox_bf[l], w_out_bf)
        h = _cross_attention(h, bsz, seq, l, norm_cross_w[l], mem_n, w_cq_bf, w_ck_bf, w_cv_bf, w_co_bf)
        last = l == n_layers - 1
        h = _moe(h, l, norm_ffn_w[l], w_group[l], b_group[l], w_router[l], b_router[l], w_eg_bf, w_eu_bf, w_ed_bf,
                 out_norm_w=final_norm_w if last else None)
    return h.reshape(bsz, seq, d)
```

```python
import functools

import jax
import jax.numpy as jnp
from jax import lax
from jax.experimental import pallas as pl
from jax.experimental.pallas import tpu as pltpu

F32 = jnp.float32
BF16 = jnp.bfloat16

D_MODEL = 4096
CONV_WIDTH = 1024
GLA_HEADS = 4
GLA_DK = 128
GLA_DV = 256
GLA_RANK = 16
GLA_TAU = 16.0
GLA_CHUNK = 64
FOX_HEADS = 16
FOX_HD = 128
CROSS_HEADS = 4
CROSS_HD = 1024
N_GROUPS = 4
EXPERTS_PER_GROUP = 8
N_EXPERTS = 32
TOP_K = 2
D_EXPERT = 512
RMS_EPS = 1e-6
LOG2_E = 1.4426950408889634
FOX_SKIP_LOG2 = 40.0

LANES = 128
BF16_SUBLANES = 16
VMEM_LIMIT = 52 * 1024 * 1024
GATHER_UNROLL = 8

COL_CONV_H, COL_CONV_B, COL_CONV_C = 0, 1024, 2048
COL_GLA_Q, COL_GLA_K, COL_GLA_V, COL_GLA_G = 3072, 3584, 4096, 5120
COL_FOX_Q, COL_FOX_K, COL_FOX_V = 6144, 8192, 10240
MAIN_COLS = 12288
SMALL_LR, SMALL_F = 0, 16


def _params(n_axes):
    return pltpu.CompilerParams(dimension_semantics=("arbitrary",) * n_axes, vmem_limit_bytes=VMEM_LIMIT)


def _log_sigmoid(x):
    return jnp.minimum(x, 0.0) - jnp.log1p(jnp.exp(-jnp.abs(x)))


def _rms_scale(x, nw):
    ms = jnp.mean(x * x, axis=-1, keepdims=True)
    return x * lax.rsqrt(ms + RMS_EPS) * nw


def _rms_cast_body(x_ref, nw_ref, o_ref):
    o_ref[...] = _rms_scale(x_ref[...], nw_ref[...]).astype(BF16)


def _rms_cast_small_body(x_ref, nw_ref, ws_ref, o_ref, os_ref):
    xn = _rms_scale(x_ref[...], nw_ref[...]).astype(BF16)
    o_ref[...] = xn
    os_ref[...] = jnp.dot(xn, ws_ref[...], preferred_element_type=F32)


def rms_cast(x, nw, w_small=None, *, tm=256):
    m, k = x.shape
    tm = min(tm, m)
    row = pl.BlockSpec((tm, k), lambda i: (i, 0))
    in_specs = [row, pl.BlockSpec((1, k), lambda i: (0, 0))]
    if w_small is None:
        return pl.pallas_call(
            _rms_cast_body, grid=(m // tm,), in_specs=in_specs, out_specs=row,
            out_shape=jax.ShapeDtypeStruct((m, k), BF16), compiler_params=_params(1), name="rms_cast",
        )(x, nw.reshape(1, k))
    return pl.pallas_call(
        _rms_cast_small_body, grid=(m // tm,),
        in_specs=in_specs + [pl.BlockSpec((k, LANES), lambda i: (0, 0))],
        out_specs=[row, pl.BlockSpec((tm, LANES), lambda i: (i, 0))],
        out_shape=[jax.ShapeDtypeStruct((m, k), BF16), jax.ShapeDtypeStruct((m, LANES), F32)],
        compiler_params=_params(1), name="rms_cast_small",
    )(x, nw.reshape(1, k), w_small)


def _matmul_body(x_ref, w_ref, o_ref):
    o_ref[...] = jnp.dot(x_ref[...], w_ref[...], preferred_element_type=F32).astype(o_ref.dtype)


def matmul(x, w, layer, *, tm=1024, tn=1024):
    m, k = x.shape
    n = w.shape[2]
    tm, tn = min(tm, m), min(tn, n)
    return pl.pallas_call(
        _matmul_body,
        grid=(m // tm, n // tn),
        in_specs=[pl.BlockSpec((tm, k), lambda i, j: (i, 0)),
                  pl.BlockSpec((None, k, tn), lambda i, j: (layer, 0, j))],
        out_specs=pl.BlockSpec((tm, tn), lambda i, j: (i, j)),
        out_shape=jax.ShapeDtypeStruct((m, n), BF16),
        compiler_params=_params(2),
        name="matmul",
    )(x, w)


def _in_proj_body(x_ref, wa_ref, wb_ref, o_ref, *, na, n_fox_q):
    j = pl.program_id(1)

    @pl.when(j < na)
    def _():
        o_ref[...] = jnp.dot(x_ref[...], wa_ref[...], preferred_element_type=F32).astype(o_ref.dtype)

    @pl.when((j >= na) & (j < na + n_fox_q))
    def _():
        q = jnp.dot(x_ref[...], wb_ref[...], preferred_element_type=F32) * (FOX_HD ** -0.5 * LOG2_E)
        o_ref[...] = q.astype(o_ref.dtype)

    @pl.when(j >= na + n_fox_q)
    def _():
        o_ref[...] = jnp.dot(x_ref[...], wb_ref[...], preferred_element_type=F32).astype(o_ref.dtype)


def in_proj(xn, w_in, layer, w_b, *, tm=1024, tn=512):
    m, k = xn.shape
    tm = min(tm, m)
    na, nb = COL_FOX_Q // tn, w_b.shape[1] // tn
    return pl.pallas_call(
        functools.partial(_in_proj_body, na=na, n_fox_q=FOX_HEADS * FOX_HD // tn),
        grid=(m // tm, na + nb),
        in_specs=[pl.BlockSpec((tm, k), lambda i, j: (i, 0)),
                  pl.BlockSpec((None, k, tn), lambda i, j: (layer, 0, jnp.minimum(j, na - 1))),
                  pl.BlockSpec((k, tn), lambda i, j: (0, jnp.maximum(j - na, 0)))],
        out_specs=pl.BlockSpec((tm, tn), lambda i, j: (i, j)),
        out_shape=jax.ShapeDtypeStruct((m, (na + nb) * tn), BF16),
        compiler_params=_params(2),
        name="in_proj",
    )(xn, w_in, w_b)


def _conv_body(h_ref, b_ref, c_ref, ph_ref, pc_ref, w_ref, o_ref):
    i = pl.program_id(1)
    u = c_ref[...].astype(F32) * h_ref[...].astype(F32)
    up = pc_ref[...].astype(F32) * ph_ref[...].astype(F32)
    up = jnp.where(i > 0, up, 0.0)
    last, last2 = up[BF16_SUBLANES - 1:BF16_SUBLANES], up[BF16_SUBLANES - 2:BF16_SUBLANES - 1]
    row = lax.broadcasted_iota(jnp.int32, u.shape, 0)
    u1 = jnp.where(row == 0, last, pltpu.roll(u, 1, 0))
    u2 = jnp.where(row == 0, last2, jnp.where(row == 1, last, pltpu.roll(u, 2, 0)))
    w = w_ref[...]
    z = w[0:1] * u2 + w[1:2] * u1 + w[2:3] * u
    o_ref[...] = (b_ref[...].astype(F32) * z).astype(BF16)


def gated_conv(proj, conv_w, bsz, seq, *, ts=512):
    ts = min(ts, seq)
    ns = seq // ts
    cw = CONV_WIDTH
    cb = lambda col: pl.BlockSpec((ts, cw), lambda b, i: (b * ns + i, col // cw))
    pb = lambda col: pl.BlockSpec(
        (BF16_SUBLANES, cw),
        lambda b, i: (jnp.maximum((b * seq + i * ts) // BF16_SUBLANES - 1, 0), col // cw))
    return pl.pallas_call(
        _conv_body,
        grid=(bsz, ns),
        in_specs=[cb(COL_CONV_H), cb(COL_CONV_B), cb(COL_CONV_C), pb(COL_CONV_H), pb(COL_CONV_C),
                  pl.BlockSpec(conv_w.shape, lambda b, i: (0, 0))],
        out_specs=pl.BlockSpec((ts, cw), lambda b, i: (b * ns + i, 0)),
        out_shape=jax.ShapeDtypeStruct((bsz * seq, cw), BF16),
        compiler_params=_params(2),
        name="gated_conv",
    )(proj, proj, proj, proj, proj, conv_w)


def _fox_c_body(x_ref, bias_ref, o_ref, carry_ref):
    @pl.when(pl.program_id(1) == 0)
    def _():
        carry_ref[...] = jnp.zeros_like(carry_ref)

    lf = _log_sigmoid(x_ref[...] + bias_ref[...])
    ts = lf.shape[0]
    row = lax.broadcasted_iota(jnp.int32, lf.shape, 0)
    k = 1
    while k < ts:
        lf = lf + jnp.where(row >= k, pltpu.roll(lf, k, 0), 0.0)
        k *= 2
    c = lf + carry_ref[...]
    o_ref[...] = c
    carry_ref[...] = c[ts - 1:ts, :]


def fox_cumulative_gate(small, bias_row, bsz, seq, *, ts=512):
    ts = min(ts, seq)
    ns = seq // ts
    return pl.pallas_call(
        _fox_c_body,
        grid=(bsz, ns),
        in_specs=[pl.BlockSpec((ts, LANES), lambda b, i: (b * ns + i, 0)),
                  pl.BlockSpec((1, LANES), lambda b, i: (0, 0))],
        out_specs=pl.BlockSpec((ts, LANES), lambda b, i: (b * ns + i, 0)),
        out_shape=jax.ShapeDtypeStruct((bsz * seq, LANES), F32),
        scratch_shapes=[pltpu.VMEM((1, LANES), F32)],
        compiler_params=_params(2),
        name="fox_cumgate",
    )(small, bias_row)


_NT = (((1,), (1,)), ((), ()))
_TN = (((0,), (0,)), ((), ()))


def _gla_body(q_ref, k_ref, v_ref, g_ref, lr_ref, wf_ref, bf_ref, nw_ref, o_ref, st_ref, b_ref):
    @pl.when(pl.program_id(1) == 0)
    def _():
        st_ref[...] = jnp.zeros_like(st_ref)

    z = jnp.dot(lr_ref[...].astype(BF16), wf_ref[...], preferred_element_type=F32) + bf_ref[...]
    la = _log_sigmoid(z) * (1.0 / GLA_TAU)
    rin = lax.broadcasted_iota(jnp.int32, la.shape, 0) & (GLA_CHUNK - 1)
    k = 1
    while k < GLA_CHUNK:
        la = la + jnp.where(rin >= k, pltpu.roll(la, k, 0), 0.0)
        k *= 2
    b_ref[...] = la

    tril = (lax.broadcasted_iota(jnp.int32, (GLA_CHUNK, GLA_CHUNK), 0)
            >= lax.broadcasted_iota(jnp.int32, (GLA_CHUNK, GLA_CHUNK), 1))

    def chunk(c, carry):
        r0 = pl.multiple_of(c * GLA_CHUNK, GLA_CHUNK)
        rows = pl.ds(r0, GLA_CHUNK)
        for h in range(GLA_HEADS):
            ks = slice(h * GLA_DK, (h + 1) * GLA_DK)
            vs = slice(h * GLA_DV, (h + 1) * GLA_DV)
            bh = b_ref[rows, ks]
            bl = bh[GLA_CHUNK - 1:GLA_CHUNK, :]
            qh = q_ref[rows, ks].astype(F32) * (GLA_DK ** -0.5)
            kh = k_ref[rows, ks].astype(F32)
            vh = v_ref[rows, vs]
            q_in = (qh * jnp.exp(bh)).astype(BF16)
            k_in = (kh * jnp.exp(-bh)).astype(BF16)
            k_out = (kh * jnp.exp(bl - bh)).astype(BF16)
            att = lax.dot_general(q_in, k_in, _NT, preferred_element_type=F32)
            att = jnp.where(tril, att, 0.0).astype(BF16)
            st = st_ref[h]
            o = (jnp.dot(att, vh, preferred_element_type=F32)
                 + lax.dot_general(q_in, st.astype(BF16), _NT, preferred_element_type=F32))
            kv = lax.dot_general(vh, k_out, _TN, preferred_element_type=F32)
            st_ref[h] = st * jnp.exp(bl) + kv
            on = _rms_scale(o, nw_ref[:, vs])
            gg = g_ref[rows, vs].astype(F32)
            o_ref[rows, vs] = (on * (gg * jax.nn.sigmoid(gg))).astype(BF16)
        return carry

    lax.fori_loop(0, b_ref.shape[0] // GLA_CHUNK, chunk, 0)


def gated_linear_attention(proj, small, wf_pad, bf_row, nw_row, bsz, seq, *, tg=512):
    tg = min(tg, seq)
    ns = seq // tg
    qk_w, v_w = GLA_HEADS * GLA_DK, GLA_HEADS * GLA_DV
    rb = lambda col, w: pl.BlockSpec((tg, w), lambda b, i: (b * ns + i, col // w))
    full = lambda a: pl.BlockSpec(a.shape, lambda b, i: (0,) * a.ndim)
    return pl.pallas_call(
        _gla_body,
        grid=(bsz, ns),
        in_specs=[rb(COL_GLA_Q, qk_w), rb(COL_GLA_K, qk_w), rb(COL_GLA_V, v_w), rb(COL_GLA_G, v_w),
                  pl.BlockSpec((tg, LANES), lambda b, i: (b * ns + i, 0)),
                  full(wf_pad), full(bf_row), full(nw_row)],
        out_specs=pl.BlockSpec((tg, v_w), lambda b, i: (b * ns + i, 0)),
        out_shape=jax.ShapeDtypeStruct((bsz * seq, v_w), BF16),
        scratch_shapes=[pltpu.VMEM((GLA_HEADS, GLA_DV, GLA_DK), F32), pltpu.VMEM((tg, qk_w), F32)],
        compiler_params=_params(2),
        name="gla",
    )(proj, proj, proj, proj, small, wf_pad, bf_row, nw_row)


def _fox_body(q_ref, k_ref, v_ref, c_ref, bound_ref, o_ref, sa_ref, sb_ref, m_ref, l_ref, acc_ref, *, tq):
    qi = pl.program_id(2)
    q = q_ref[...]
    c0 = c_ref[pl.ds(qi, 1), :][:, 0:1]
    m_ref[...] = jnp.full_like(m_ref, -jnp.inf)
    l_ref[...] = jnp.zeros_like(l_ref)
    acc_ref[...] = jnp.zeros_like(acc_ref)
    n_lane_tiles = tq // LANES

    def key_rows(j):
        return pl.ds(pl.multiple_of(j * tq, tq), tq)

    def scores(j, s_ref):
        s_ref[...] = lax.dot_general(q, k_ref[key_rows(j), :], _NT, preferred_element_type=F32)

    def update(j, s_ref, masked):
        s = s_ref[...] - (c_ref[pl.ds(j, 1), :] - c0) * LOG2_E
        if masked:
            keep = (lax.broadcasted_iota(jnp.int32, s.shape, 1) <= lax.broadcasted_iota(jnp.int32, s.shape, 0))
            s = jnp.where(keep, s, -jnp.inf)
        m_prev = m_ref[...]
        m_new = jnp.maximum(m_prev, jnp.max(s, axis=-1, keepdims=True))
        p = jnp.concatenate([jnp.exp2(s[:, t * LANES:(t + 1) * LANES] - m_new) for t in range(n_lane_tiles)], axis=1)
        alpha = jnp.exp2(m_prev - m_new)
        l_ref[...] = alpha * l_ref[...] + jnp.sum(p, axis=-1, keepdims=True)
        acc_ref[...] = alpha * acc_ref[...] + jnp.dot(p.astype(BF16), v_ref[key_rows(j), :],
                                                      preferred_element_type=F32)
        m_ref[...] = m_new

    scores(qi, sa_ref)
    scores(jnp.maximum(qi - 1, 0), sb_ref)
    update(qi, sa_ref, True)

    m_min = jnp.min(m_ref[...], axis=0, keepdims=True)[:, 0:1]
    lane = lax.broadcasted_iota(jnp.int32, (1, LANES), 1)
    negligible = (bound_ref[pl.ds(qi, 1), :] + FOX_SKIP_LOG2 < m_min) & (lane < qi)
    j0 = jnp.sum(negligible.astype(jnp.int32))
    n_tiles = qi - j0

    def pair(t, carry):
        j = qi - 1 - 2 * t
        scores(jnp.maximum(j - 1, 0), sa_ref)
        update(j, sb_ref, False)
        scores(jnp.maximum(j - 2, 0), sb_ref)
        update(j - 1, sa_ref, False)
        return carry

    lax.fori_loop(0, n_tiles // 2, pair, 0)

    @pl.when(n_tiles % 2 == 1)
    def _():
        update(j0, sb_ref, False)

    o_ref[...] = (acc_ref[...] / l_ref[...]).astype(BF16)


def _fox_norm_body(q_ref, k_ref, qn_ref, kn_ref):
    lane = lax.broadcasted_iota(jnp.int32, (1, LANES), 1)

    def max_row_norms(ref):
        out = jnp.zeros((1, LANES), F32)
        for h in range(FOX_HEADS):
            x = ref[:, h * FOX_HD:(h + 1) * FOX_HD].astype(F32)
            sq = jnp.max(jnp.sum(x * x, axis=-1, keepdims=True), axis=0, keepdims=True)
            out = jnp.where(lane == h, jnp.sqrt(sq), out)
        return out

    qn_ref[...] = max_row_norms(q_ref)
    kn_ref[...] = max_row_norms(k_ref)


def fox_tile_norms(proj, n_tiles, tq):
    w = FOX_HEADS * FOX_HD
    out = jax.ShapeDtypeStruct((n_tiles, 1, LANES), F32)
    return pl.pallas_call(
        _fox_norm_body,
        grid=(n_tiles,),
        in_specs=[pl.BlockSpec((tq, w), lambda i: (i, COL_FOX_Q // w)),
                  pl.BlockSpec((tq, w), lambda i: (i, COL_FOX_K // w))],
        out_specs=[pl.BlockSpec((None, 1, LANES), lambda i: (i, 0, 0))] * 2,
        out_shape=[out, out],
        compiler_params=_params(1),
        name="fox_tile_norms",
    )(proj, proj)


def forgetting_attention(proj, c_blocks, bsz, seq, *, tq=512):
    tq = min(tq, seq)
    nq = seq // tq
    hd = FOX_HD
    qn, kn = fox_tile_norms(proj, bsz * nq, tq)
    per_head = lambda a: a[:, 0, :FOX_HEADS].reshape(bsz, nq, FOX_HEADS).transpose(0, 2, 1)
    kn_run = lax.cummax(per_head(kn), axis=2)
    c_first, c_last = c_blocks[..., 0], c_blocks[..., -1]
    bound = (per_head(qn)[..., :, None] * kn_run[..., None, :]
             - (c_last[..., None, :] - c_first[..., :, None]) * LOG2_E)
    bound = jnp.pad(bound, ((0, 0), (0, 0), (0, 0), (0, LANES - nq)))
    return pl.pallas_call(
        functools.partial(_fox_body, tq=tq),
        grid=(bsz, FOX_HEADS, nq),
        in_specs=[pl.BlockSpec((tq, hd), lambda b, h, i: (b * nq + i, COL_FOX_Q // hd + h)),
                  pl.BlockSpec((seq, hd), lambda b, h, i: (b, COL_FOX_K // hd + h)),
                  pl.BlockSpec((seq, hd), lambda b, h, i: (b, COL_FOX_V // hd + h)),
                  pl.BlockSpec((None, None, nq, tq), lambda b, h, i: (b, h, 0, 0)),
                  pl.BlockSpec((None, None, nq, LANES), lambda b, h, i: (b, h, 0, 0))],
        out_specs=pl.BlockSpec((tq, hd), lambda b, h, i: (b * nq + i, h)),
        out_shape=jax.ShapeDtypeStruct((bsz * seq, FOX_HEADS * hd), BF16),
        scratch_shapes=[pltpu.VMEM((tq, tq), F32), pltpu.VMEM((tq, tq), F32),
                        pltpu.VMEM((tq, LANES), F32), pltpu.VMEM((tq, LANES), F32), pltpu.VMEM((tq, hd), F32)],
        compiler_params=_params(3),
        name="fox",
    )(proj, proj, proj, c_blocks, bound)


def _mix_out_body(yc_ref, yg_ref, yf_ref, wc_ref, wg_ref, wf_ref, r_ref, o_ref):
    acc = jnp.dot(yc_ref[...], wc_ref[...], preferred_element_type=F32)
    acc += jnp.dot(yg_ref[...], wg_ref[...], preferred_element_type=F32)
    acc += jnp.dot(yf_ref[...], wf_ref[...], preferred_element_type=F32)
    o_ref[...] = r_ref[...] + acc


def mixer_out_proj(y_conv, y_gla, y_fox, w_out, layer, res, *, tm=1024, tn=512):
    m = res.shape[0]
    n = w_out.shape[2]
    tm = min(tm, m)
    kc, kg, kf = y_conv.shape[1], y_gla.shape[1], y_fox.shape[1]
    return pl.pallas_call(
        _mix_out_body,
        grid=(m // tm, n // tn),
        in_specs=[pl.BlockSpec((tm, kc), lambda i, j: (i, 0)),
                  pl.BlockSpec((tm, kg), lambda i, j: (i, 0)),
                  pl.BlockSpec((tm, kf), lambda i, j: (i, 0)),
                  pl.BlockSpec((None, kc, tn), lambda i, j: (layer, 0, j)),
                  pl.BlockSpec((None, kg, tn), lambda i, j: (layer, kc // kg, j)),
                  pl.BlockSpec((None, kf, tn), lambda i, j: (layer, (kc + kg) // kf, j)),
                  pl.BlockSpec((tm, tn), lambda i, j: (i, j))],
        out_specs=pl.BlockSpec((tm, tn), lambda i, j: (i, j)),
        out_shape=jax.ShapeDtypeStruct((m, n), F32),
        compiler_params=_params(2),
        name="mixer_out_proj",
    )(y_conv, y_gla, y_fox, w_out, w_out, w_out, res)


def _cross_body(q_ref, kt_ref, v_ref, w_ref, r_ref, o_ref, att_ref):
    @pl.when(pl.program_id(1) == 0)
    def _():
        for h in range(CROSS_HEADS):
            hs = slice(h * CROSS_HD, (h + 1) * CROSS_HD)
            s = jnp.dot(q_ref[:, hs], kt_ref[hs, :], preferred_element_type=F32) * (CROSS_HD ** -0.5)
            p = jnp.exp(s - jnp.max(s, axis=-1, keepdims=True))
            p = p / jnp.sum(p, axis=-1, keepdims=True)
            att_ref[:, hs] = jnp.dot(p.astype(BF16), v_ref[:, hs], preferred_element_type=F32).astype(BF16)

    o_ref[...] = r_ref[...] + jnp.dot(att_ref[...], w_ref[...], preferred_element_type=F32)


def cross_attention_out(q, k_t, v, w_co, layer, res, seq, *, tm=1024, tn=512):
    m, d = q.shape
    mem_len = v.shape[1]
    tm = min(tm, seq)
    return pl.pallas_call(
        _cross_body,
        grid=(m // tm, d // tn),
        in_specs=[pl.BlockSpec((tm, d), lambda i, j: (i, 0)),
                  pl.BlockSpec((None, d, mem_len), lambda i, j: ((i * tm) // seq, 0, 0)),
                  pl.BlockSpec((None, mem_len, d), lambda i, j: ((i * tm) // seq, 0, 0)),
                  pl.BlockSpec((None, d, tn), lambda i, j: (layer, 0, j)),
                  pl.BlockSpec((tm, tn), lambda i, j: (i, j))],
        out_specs=pl.BlockSpec((tm, tn), lambda i, j: (i, j)),
        out_shape=jax.ShapeDtypeStruct((m, d), F32),
        scratch_shapes=[pltpu.VMEM((tm, d), BF16)],
        compiler_params=_params(2),
        name="cross_attention_out",
    )(q, k_t, v, w_co, res)


def _router_body(x_ref, nw_ref, whi_ref, wlo_ref, xn_ref, lg_ref):
    xn = _rms_scale(x_ref[...], nw_ref[...])
    xn_ref[...] = xn
    hi = xn.astype(BF16)
    lo = (xn - hi.astype(F32)).astype(BF16)
    lg_ref[...] = (jnp.dot(hi, whi_ref[...], preferred_element_type=F32)
                   + (jnp.dot(hi, wlo_ref[...], preferred_element_type=F32)
                      + jnp.dot(lo, whi_ref[...], preferred_element_type=F32)))


def moe_router(x, nw, w_hi, w_lo, *, tm=256):
    m, k = x.shape
    tm = min(tm, m)
    return pl.pallas_call(
        _router_body,
        grid=(m // tm,),
        in_specs=[pl.BlockSpec((tm, k), lambda i: (i, 0)),
                  pl.BlockSpec((1, k), lambda i: (0, 0)),
                  pl.BlockSpec((k, LANES), lambda i: (0, 0)),
                  pl.BlockSpec((k, LANES), lambda i: (0, 0))],
        out_specs=[pl.BlockSpec((tm, k), lambda i: (i, 0)),
                   pl.BlockSpec((tm, LANES), lambda i: (i, 0))],
        out_shape=[jax.ShapeDtypeStruct((m, k), F32), jax.ShapeDtypeStruct((m, LANES), F32)],
        compiler_params=_params(1),
        name="moe_router",
    )(x, nw.reshape(1, k), w_hi, w_lo)


def _row_copy(src_hbm, src_row, dst, dst_row, sem):
    return pltpu.make_async_copy(src_hbm.at[pl.ds(src_row, 1)], dst.at[pl.ds(dst_row, 1)], sem)


def _experts_body(be_ref, tok_ref, nused_ref, x_hbm, wg_ref, wu_ref, wd_ref, y_ref, xbuf, h_ref, sem, *, rb, n_blocks):
    i = pl.program_id(0)
    n_used = nused_ref[0]
    d = y_ref.shape[1]
    de = h_ref.shape[1]

    def start_rows(blk, slot, r0, r1):
        for r in range(r0, r1):
            _row_copy(x_hbm, tok_ref[blk * rb + r], xbuf.at[slot], r, sem.at[slot]).start()

    def wait_gather(slot):
        for r in range(rb):
            _row_copy(x_hbm, 0, xbuf.at[slot], r, sem.at[slot]).wait()

    @pl.when((i == 0) & (n_used > 0))
    def _():
        def body(r, carry):
            _row_copy(x_hbm, tok_ref[r], xbuf.at[0], r, sem.at[0]).start()
            return carry
        lax.fori_loop(0, rb, body, 0, unroll=GATHER_UNROLL)

    @pl.when(i < n_used)
    def _():
        slot = i % 2
        nxt = 1 - slot
        wait_gather(slot)
        x = xbuf[slot].astype(BF16)
        n_up = 2
        n_down = d // de
        per_up, per_down = rb // 4, rb // (2 * n_down)
        r = 0
        for c in range(n_up):
            start_rows(i + 1, nxt, r, r + per_up)
            r += per_up
            cols = slice(c * de // n_up, (c + 1) * de // n_up)
            g = jnp.dot(x, wg_ref[:, cols], preferred_element_type=F32)
            u = jnp.dot(x, wu_ref[:, cols], preferred_element_type=F32)
            h_ref[:, cols] = (g * jax.nn.sigmoid(g) * u).astype(BF16)
        for c in range(n_down):
            start_rows(i + 1, nxt, r, r + per_down)
            r += per_down
            cols = slice(c * de, (c + 1) * de)
            y_ref[:, cols] = jnp.dot(h_ref[...], wd_ref[:, cols], preferred_element_type=F32)

    @pl.when((i == n_used) & (n_used > 0))
    def _():
        wait_gather(i % 2)

    @pl.when((i >= n_used) & (i < n_blocks))
    def _():
        y_ref[...] = jnp.zeros_like(y_ref)


def moe_experts(xn, row_tok, block_expert, n_used, w_gate, w_up, w_down, layer, *, rb):
    d = xn.shape[1]
    n_blocks = row_tok.shape[0] // rb - 1
    de = w_gate.shape[3]
    blk = lambda i: jnp.minimum(i, n_blocks - 1)
    grid_spec = pltpu.PrefetchScalarGridSpec(
        num_scalar_prefetch=3,
        grid=(n_blocks + 1,),
        in_specs=[pl.BlockSpec(memory_space=pl.ANY),
                  pl.BlockSpec((None, None, d, de), lambda i, be, tok, nu: (layer, be[blk(i)], 0, 0)),
                  pl.BlockSpec((None, None, d, de), lambda i, be, tok, nu: (layer, be[blk(i)], 0, 0)),
                  pl.BlockSpec((None, None, de, d), lambda i, be, tok, nu: (layer, be[blk(i)], 0, 0))],
        out_specs=pl.BlockSpec((rb, d), lambda i, be, tok, nu: (blk(i), 0)),
        scratch_shapes=[pltpu.VMEM((2, rb, d), F32), pltpu.VMEM((rb, de), BF16), pltpu.SemaphoreType.DMA((2,))],
    )
    return pl.pallas_call(
        functools.partial(_experts_body, rb=rb, n_blocks=n_blocks),
        grid_spec=grid_spec,
        out_shape=jax.ShapeDtypeStruct((n_blocks * rb, d), F32),
        compiler_params=_params(1),
        name="moe_experts",
    )(block_expert, row_tok, n_used, xn, w_gate, w_up, w_down)


def _combine_body(dest_ref, h_ref, g_ref, nw_ref, y_hbm, o_ref, buf, sem, *, tc, normalize):
    i = pl.program_id(0)

    def start_gather(tile, slot):
        def body(t, carry):
            for k in range(TOP_K):
                _row_copy(y_hbm, dest_ref[(tile * tc + t) * TOP_K + k], buf.at[slot, k], t, sem.at[slot]).start()
            return carry
        lax.fori_loop(0, tc, body, 0, unroll=GATHER_UNROLL // TOP_K)

    @pl.when(i == 0)
    def _():
        start_gather(0, 0)

    @pl.when(i + 1 < pl.num_programs(0))
    def _():
        start_gather(i + 1, (i + 1) % 2)

    slot = i % 2
    for t in range(tc):
        for k in range(TOP_K):
            _row_copy(y_hbm, 0, buf.at[slot, k], t, sem.at[slot]).wait()
    g = g_ref[...]
    out = h_ref[...] + (buf[slot, 0] * g[:, 0:1] + buf[slot, 1] * g[:, 1:2])
    o_ref[...] = _rms_scale(out, nw_ref[...]) if normalize else out


def moe_combine(h, y_rows, dest, gates, out_norm_w=None, *, tc=128):
    t, d = h.shape
    tc = min(tc, t)
    normalize = out_norm_w is not None
    nw = (out_norm_w if normalize else jnp.ones((d,), F32)).reshape(1, d)
    grid_spec = pltpu.PrefetchScalarGridSpec(
        num_scalar_prefetch=1,
        grid=(t // tc,),
        in_specs=[pl.BlockSpec((tc, d), lambda i, dest: (i, 0)),
                  pl.BlockSpec((tc, TOP_K), lambda i, dest: (i, 0)),
                  pl.BlockSpec((1, d), lambda i, dest: (0, 0)),
                  pl.BlockSpec(memory_space=pl.ANY)],
        out_specs=pl.BlockSpec((tc, d), lambda i, dest: (i, 0)),
        scratch_shapes=[pltpu.VMEM((2, TOP_K, tc, d), F32), pltpu.SemaphoreType.DMA((2,))],
    )
    return pl.pallas_call(
        functools.partial(_combine_body, tc=tc, normalize=normalize),
        grid_spec=grid_spec,
        out_shape=jax.ShapeDtypeStruct((t, d), F32),
        compiler_params=_params(1),
        name="moe_combine",
    )(dest, h, gates, nw, y_rows)


def _routing_tables(logits, b_group, b_router, rb):
    t = logits.shape[0]
    group_logits = logits[:, :N_GROUPS] + b_group
    group = jnp.argmax(group_logits, axis=-1)
    p_group = jnp.take_along_axis(jax.nn.softmax(group_logits, axis=-1), group[:, None], axis=-1)[:, 0]
    exp_logits = (logits[:, N_GROUPS:N_GROUPS + N_EXPERTS] + b_router).reshape(t, N_GROUPS, EXPERTS_PER_GROUP)
    in_group = jnp.take_along_axis(exp_logits, group[:, None, None], axis=1)[:, 0]
    top_p, top_e = lax.top_k(jax.nn.softmax(in_group, axis=-1), TOP_K)
    gate = (p_group[:, None] * top_p / jnp.sum(top_p, axis=-1, keepdims=True)).reshape(-1)
    eid = (group[:, None] * EXPERTS_PER_GROUP + top_e).reshape(-1).astype(jnp.int32)

    n_assign = t * TOP_K
    seg = min(512, n_assign)
    onehot = (eid[:, None] == jnp.arange(N_EXPERTS, dtype=jnp.int32)[None, :]).astype(F32)
    within = jnp.einsum("ij,bjk->bik", jnp.tril(jnp.ones((seg, seg), F32)),
                        onehot.reshape(n_assign // seg, seg, N_EXPERTS), preferred_element_type=F32)
    seg_total = within[:, -1, :]
    seg_start = jnp.cumsum(seg_total, axis=0) - seg_total
    running = (within + seg_start[:, None, :]).reshape(n_assign, N_EXPERTS)
    counts = (seg_start[-1] + seg_total[-1]).astype(jnp.int32)
    rank = jnp.sum(running * onehot, axis=1).astype(jnp.int32) - 1
    padded = (counts + rb - 1) // rb * rb
    padded_end = jnp.cumsum(padded)
    dest = ((padded_end - padded)[eid] + rank).astype(jnp.int32)
    n_rows = n_assign + N_EXPERTS * rb
    n_blocks = n_rows // rb
    row_tok = jnp.zeros((n_rows + rb,), jnp.int32).at[dest].set(jnp.arange(n_assign, dtype=jnp.int32) // TOP_K)
    block_row0 = jnp.arange(n_blocks, dtype=jnp.int32) * rb
    block_expert = jnp.minimum(jnp.sum((padded_end[None, :] <= block_row0[:, None]).astype(jnp.int32), axis=1),
                               N_EXPERTS - 1)
    n_used = (padded_end[-1] // rb).astype(jnp.int32).reshape(1)
    return dest, row_tok, gate.reshape(t, TOP_K), block_expert, n_used


def _pad_lanes(a):
    return jnp.pad(a, ((0, 0), (0, LANES - a.shape[1])))


def _hybrid_mixer(h, bsz, seq, layer, norm_w, w_in_f32, w_in_bf, conv_w, gla_wf2, gla_bf, gla_norm_w, fox_bf, w_out_bf):
    glr0 = COL_GLA_G + GLA_HEADS * GLA_DV
    ff0 = w_in_f32.shape[2] - FOX_HEADS
    fq0 = glr0 + GLA_RANK
    w_b = w_in_bf[layer, :, fq0:ff0]
    w_small = _pad_lanes(jnp.concatenate([w_in_f32[layer, :, glr0:fq0], w_in_f32[layer, :, ff0:]], axis=1)).astype(BF16)
    xn, small = rms_cast(h, norm_w, w_small)
    proj = in_proj(xn, w_in_bf, layer, w_b)

    y_conv = gated_conv(proj, conv_w, bsz, seq)

    wf_pad = jnp.pad(gla_wf2, ((SMALL_LR, LANES - SMALL_LR - GLA_RANK), (0, 0))).astype(BF16)
    y_gla = gated_linear_attention(proj, small, wf_pad, gla_bf.reshape(1, -1), gla_norm_w.reshape(1, -1), bsz, seq)

    bias_row = jnp.pad(fox_bf, (SMALL_F, LANES - SMALL_F - FOX_HEADS)).reshape(1, LANES)
    c = fox_cumulative_gate(small, bias_row, bsz, seq)
    tq = min(512, seq)
    c_blocks = (c[:, SMALL_F:SMALL_F + FOX_HEADS].reshape(bsz, seq, FOX_HEADS)
                .transpose(0, 2, 1).reshape(bsz, FOX_HEADS, seq // tq, tq))
    y_fox = forgetting_attention(proj, c_blocks, bsz, seq, tq=tq)

    return mixer_out_proj(y_conv, y_gla, y_fox, w_out_bf, layer, h)


def _cross_attention(h, bsz, seq, layer, norm_w, mem_n, w_cq, w_ck, w_cv, w_co):
    mem_len = mem_n.shape[0] // bsz
    q = matmul(rms_cast(h, norm_w), w_cq, layer)
    k = matmul(mem_n, w_ck, layer)
    v = matmul(mem_n, w_cv, layer)
    k_t = k.reshape(bsz, mem_len, D_MODEL).transpose(0, 2, 1)
    return cross_attention_out(q, k_t, v.reshape(bsz, mem_len, D_MODEL), w_co, layer, h, seq)


def _moe(h, layer, norm_w, w_group, b_group, w_router, b_router, w_gate, w_up, w_down, *, out_norm_w=None, rb=256):
    w_r = _pad_lanes(jnp.concatenate([w_group, w_router], axis=1))
    w_hi = w_r.astype(BF16)
    w_lo = (w_r - w_hi.astype(F32)).astype(BF16)
    xn, logits = moe_router(h, norm_w, w_hi, w_lo)
    dest, row_tok, gates, block_expert, n_used = _routing_tables(logits, b_group, b_router, rb)
    y_rows = moe_experts(xn, row_tok, block_expert, n_used, w_gate, w_up, w_down, layer, rb=rb)
    return moe_combine(h, y_rows, dest, gates, out_norm_w)


def kernel(x, mem, norm_mix_w, w_in, conv_w, gla_wf2, gla_bf, gla_norm_w, fox_bf, w_out, norm_cross_w, mem_norm_w, w_cq, w_ck, w_cv, w_co, norm_ffn_w, w_group, b_group, w_router, b_router, w_expert_gate, w_expert_up, w_expert_down, final_norm_w):
    bsz, seq, d = x.shape
    h = x.reshape(bsz * seq, d)
    mem_n = rms_cast(mem.reshape(-1, d), mem_norm_w)
    n_layers = norm_mix_w.shape[0]
    w_in_bf, w_out_bf = w_in.astype(BF16), w_out.astype(BF16)
    w_cq_bf, w_ck_bf, w_cv_bf, w_co_bf = (w.astype(BF16) for w in (w_cq, w_ck, w_cv, w_co))
    w_eg_bf, w_eu_bf, w_ed_bf = (w.astype(BF16) for w in (w_expert_gate, w_expert_up, w_expert_down))
    for l in range(n_layers):
        h = _hybrid_mixer(h, bsz, seq, l, norm_mix_w[l], w_in, w_in_bf, conv_w[l], gla_wf2[l], gla_bf[l],
                          gla_norm_w[l], fox_bf[l], w_out_bf)
        h = _cross_attention(h, bsz, seq, l, norm_cross_w[l], mem_n, w_cq_bf, w_ck_bf, w_cv_bf, w_co_bf)
        last = l == n_layers - 1
        h = _moe(h, l, norm_ffn_w[l], w_group[l], b_group[l], w_router[l], b_router[l], w_eg_bf, w_eu_bf, w_ed_bf,
                 out_norm_w=final_norm_w if last else None)
    return h.reshape(bsz, seq, d)
```

```python
import functools

import jax
import jax.numpy as jnp
from jax import lax
from jax.experimental import pallas as pl
from jax.experimental.pallas import tpu as pltpu

F32 = jnp.float32
BF16 = jnp.bfloat16

D_MODEL = 4096
CONV_WIDTH = 1024
GLA_HEADS = 4
GLA_DK = 128
GLA_DV = 256
GLA_RANK = 16
GLA_TAU = 16.0
GLA_CHUNK = 64
FOX_HEADS = 16
FOX_HD = 128
CROSS_HEADS = 4
CROSS_HD = 1024
N_GROUPS = 4
EXPERTS_PER_GROUP = 8
N_EXPERTS = 32
TOP_K = 2
D_EXPERT = 512
RMS_EPS = 1e-6
LOG2_E = 1.4426950408889634
FOX_SKIP_LOG2 = 40.0

LANES = 128
BF16_SUBLANES = 16
VMEM_LIMIT = 52 * 1024 * 1024
GATHER_UNROLL = 8

COL_CONV_H, COL_CONV_B, COL_CONV_C = 0, 1024, 2048
COL_GLA_Q, COL_GLA_K, COL_GLA_V, COL_GLA_G = 3072, 3584, 4096, 5120
COL_FOX_Q, COL_FOX_K, COL_FOX_V = 6144, 8192, 10240
MAIN_COLS = 12288
SMALL_LR, SMALL_F = 0, 16


def _params(n_axes):
    return pltpu.CompilerParams(dimension_semantics=("arbitrary",) * n_axes, vmem_limit_bytes=VMEM_LIMIT)


def _log_sigmoid(x):
    return jnp.minimum(x, 0.0) - jnp.log1p(jnp.exp(-jnp.abs(x)))


def _rms_scale(x, nw):
    ms = jnp.mean(x * x, axis=-1, keepdims=True)
    return x * lax.rsqrt(ms + RMS_EPS) * nw


def _rms_cast_body(x_ref, nw_ref, o_ref):
    o_ref[...] = _rms_scale(x_ref[...], nw_ref[...]).astype(BF16)


def _rms_cast_small_body(x_ref, nw_ref, ws_ref, o_ref, os_ref):
    xn = _rms_scale(x_ref[...], nw_ref[...]).astype(BF16)
    o_ref[...] = xn
    os_ref[...] = jnp.dot(xn, ws_ref[...], preferred_element_type=F32)


def rms_cast(x, nw, w_small=None, *, tm=256):
    m, k = x.shape
    tm = min(tm, m)
    row = pl.BlockSpec((tm, k), lambda i: (i, 0))
    in_specs = [row, pl.BlockSpec((1, k), lambda i: (0, 0))]
    if w_small is None:
        return pl.pallas_call(
            _rms_cast_body, grid=(m // tm,), in_specs=in_specs, out_specs=row,
            out_shape=jax.ShapeDtypeStruct((m, k), BF16), compiler_params=_params(1), name="rms_cast",
        )(x, nw.reshape(1, k))
    return pl.pallas_call(
        _rms_cast_small_body, grid=(m // tm,),
        in_specs=in_specs + [pl.BlockSpec((k, LANES), lambda i: (0, 0))],
        out_specs=[row, pl.BlockSpec((tm, LANES), lambda i: (i, 0))],
        out_shape=[jax.ShapeDtypeStruct((m, k), BF16), jax.ShapeDtypeStruct((m, LANES), F32)],
        compiler_params=_params(1), name="rms_cast_small",
    )(x, nw.reshape(1, k), w_small)


def _matmul_body(x_ref, w_ref, o_ref):
    o_ref[...] = jnp.dot(x_ref[...], w_ref[...].astype(BF16), preferred_element_type=F32).astype(o_ref.dtype)


def matmul(x, w, layer, *, tm=1024, tn=1024):
    m, k = x.shape
    n = w.shape[2]
    tm, tn = min(tm, m), min(tn, n)
    if w.dtype != BF16:
        assert m == tm, "f32 weights are only read once: single row tile"
        tn = min(tn, 512)
    return pl.pallas_call(
        _matmul_body,
        grid=(m // tm, n // tn),
        in_specs=[pl.BlockSpec((tm, k), lambda i, j: (i, 0)),
                  pl.BlockSpec((None, k, tn), lambda i, j: (layer, 0, j))],
        out_specs=pl.BlockSpec((tm, tn), lambda i, j: (i, j)),
        out_shape=jax.ShapeDtypeStruct((m, n), BF16),
        compiler_params=_params(2),
        name="matmul",
    )(x, w)


def _in_proj_body(x_ref, wa_ref, wb_ref, o_ref, *, na, n_fox_q):
    j = pl.program_id(1)

    @pl.when(j < na)
    def _():
        o_ref[...] = jnp.dot(x_ref[...], wa_ref[...], preferred_element_type=F32).astype(o_ref.dtype)

    @pl.when((j >= na) & (j < na + n_fox_q))
    def _():
        q = jnp.dot(x_ref[...], wb_ref[...], preferred_element_type=F32) * (FOX_HD ** -0.5 * LOG2_E)
        o_ref[...] = q.astype(o_ref.dtype)

    @pl.when(j >= na + n_fox_q)
    def _():
        o_ref[...] = jnp.dot(x_ref[...], wb_ref[...], preferred_element_type=F32).astype(o_ref.dtype)


def in_proj(xn, w_in, layer, w_b, *, tm=1024, tn=512):
    m, k = xn.shape
    tm = min(tm, m)
    na, nb = COL_FOX_Q // tn, w_b.shape[1] // tn
    return pl.pallas_call(
        functools.partial(_in_proj_body, na=na, n_fox_q=FOX_HEADS * FOX_HD // tn),
        grid=(m // tm, na + nb),
        in_specs=[pl.BlockSpec((tm, k), lambda i, j: (i, 0)),
                  pl.BlockSpec((None, k, tn), lambda i, j: (layer, 0, jnp.minimum(j, na - 1))),
                  pl.BlockSpec((k, tn), lambda i, j: (0, jnp.maximum(j - na, 0)))],
        out_specs=pl.BlockSpec((tm, tn), lambda i, j: (i, j)),
        out_shape=jax.ShapeDtypeStruct((m, (na + nb) * tn), BF16),
        compiler_params=_params(2),
        name="in_proj",
    )(xn, w_in, w_b)


def _conv_body(h_ref, b_ref, c_ref, ph_ref, pc_ref, w_ref, o_ref):
    i = pl.program_id(1)
    u = c_ref[...].astype(F32) * h_ref[...].astype(F32)
    up = pc_ref[...].astype(F32) * ph_ref[...].astype(F32)
    up = jnp.where(i > 0, up, 0.0)
    last, last2 = up[BF16_SUBLANES - 1:BF16_SUBLANES], up[BF16_SUBLANES - 2:BF16_SUBLANES - 1]
    row = lax.broadcasted_iota(jnp.int32, u.shape, 0)
    u1 = jnp.where(row == 0, last, pltpu.roll(u, 1, 0))
    u2 = jnp.where(row == 0, last2, jnp.where(row == 1, last, pltpu.roll(u, 2, 0)))
    w = w_ref[...]
    z = w[0:1] * u2 + w[1:2] * u1 + w[2:3] * u
    o_ref[...] = (b_ref[...].astype(F32) * z).astype(BF16)


def gated_conv(proj, conv_w, bsz, seq, *, ts=512):
    ts = min(ts, seq)
    ns = seq // ts
    cw = CONV_WIDTH
    cb = lambda col: pl.BlockSpec((ts, cw), lambda b, i: (b * ns + i, col // cw))
    pb = lambda col: pl.BlockSpec(
        (BF16_SUBLANES, cw),
        lambda b, i: (jnp.maximum((b * seq + i * ts) // BF16_SUBLANES - 1, 0), col // cw))
    return pl.pallas_call(
        _conv_body,
        grid=(bsz, ns),
        in_specs=[cb(COL_CONV_H), cb(COL_CONV_B), cb(COL_CONV_C), pb(COL_CONV_H), pb(COL_CONV_C),
                  pl.BlockSpec(conv_w.shape, lambda b, i: (0, 0))],
        out_specs=pl.BlockSpec((ts, cw), lambda b, i: (b * ns + i, 0)),
        out_shape=jax.ShapeDtypeStruct((bsz * seq, cw), BF16),
        compiler_params=_params(2),
        name="gated_conv",
    )(proj, proj, proj, proj, proj, conv_w)


def _fox_c_body(x_ref, bias_ref, o_ref, carry_ref):
    @pl.when(pl.program_id(1) == 0)
    def _():
        carry_ref[...] = jnp.zeros_like(carry_ref)

    lf = _log_sigmoid(x_ref[...] + bias_ref[...])
    ts = lf.shape[0]
    row = lax.broadcasted_iota(jnp.int32, lf.shape, 0)
    k = 1
    while k < ts:
        lf = lf + jnp.where(row >= k, pltpu.roll(lf, k, 0), 0.0)
        k *= 2
    c = lf + carry_ref[...]
    o_ref[...] = c
    carry_ref[...] = c[ts - 1:ts, :]


def fox_cumulative_gate(small, bias_row, bsz, seq, *, ts=512):
    ts = min(ts, seq)
    ns = seq // ts
    return pl.pallas_call(
        _fox_c_body,
        grid=(bsz, ns),
        in_specs=[pl.BlockSpec((ts, LANES), lambda b, i: (b * ns + i, 0)),
                  pl.BlockSpec((1, LANES), lambda b, i: (0, 0))],
        out_specs=pl.BlockSpec((ts, LANES), lambda b, i: (b * ns + i, 0)),
        out_shape=jax.ShapeDtypeStruct((bsz * seq, LANES), F32),
        scratch_shapes=[pltpu.VMEM((1, LANES), F32)],
        compiler_params=_params(2),
        name="fox_cumgate",
    )(small, bias_row)


_NT = (((1,), (1,)), ((), ()))
_TN = (((0,), (0,)), ((), ()))


def _gla_body(q_ref, k_ref, v_ref, g_ref, lr_ref, wf_ref, bf_ref, nw_ref, o_ref, st_ref, b_ref):
    @pl.when(pl.program_id(1) == 0)
    def _():
        st_ref[...] = jnp.zeros_like(st_ref)

    z = jnp.dot(lr_ref[...].astype(BF16), wf_ref[...], preferred_element_type=F32) + bf_ref[...]
    la = _log_sigmoid(z) * (1.0 / GLA_TAU)
    rin = lax.broadcasted_iota(jnp.int32, la.shape, 0) & (GLA_CHUNK - 1)
    k = 1
    while k < GLA_CHUNK:
        la = la + jnp.where(rin >= k, pltpu.roll(la, k, 0), 0.0)
        k *= 2
    b_ref[...] = la

    tril = (lax.broadcasted_iota(jnp.int32, (GLA_CHUNK, GLA_CHUNK), 0)
            >= lax.broadcasted_iota(jnp.int32, (GLA_CHUNK, GLA_CHUNK), 1))

    def chunk(c, carry):
        r0 = pl.multiple_of(c * GLA_CHUNK, GLA_CHUNK)
        rows = pl.ds(r0, GLA_CHUNK)
        for h in range(GLA_HEADS):
            ks = slice(h * GLA_DK, (h + 1) * GLA_DK)
            vs = slice(h * GLA_DV, (h + 1) * GLA_DV)
            bh = b_ref[rows, ks]
            bl = bh[GLA_CHUNK - 1:GLA_CHUNK, :]
            qh = q_ref[rows, ks].astype(F32) * (GLA_DK ** -0.5)
            kh = k_ref[rows, ks].astype(F32)
            vh = v_ref[rows, vs]
            q_in = (qh * jnp.exp(bh)).astype(BF16)
            k_in = (kh * jnp.exp(-bh)).astype(BF16)
            k_out = (kh * jnp.exp(bl - bh)).astype(BF16)
            att = lax.dot_general(q_in, k_in, _NT, preferred_element_type=F32)
            att = jnp.where(tril, att, 0.0).astype(BF16)
            st = st_ref[h]
            o = (jnp.dot(att, vh, preferred_element_type=F32)
                 + lax.dot_general(q_in, st.astype(BF16), _NT, preferred_element_type=F32))
            kv = lax.dot_general(vh, k_out, _TN, preferred_element_type=F32)
            st_ref[h] = st * jnp.exp(bl) + kv
            on = _rms_scale(o, nw_ref[:, vs])
            gg = g_ref[rows, vs].astype(F32)
            o_ref[rows, vs] = (on * (gg * jax.nn.sigmoid(gg))).astype(BF16)
        return carry

    lax.fori_loop(0, b_ref.shape[0] // GLA_CHUNK, chunk, 0)


def gated_linear_attention(proj, small, wf_pad, bf_row, nw_row, bsz, seq, *, tg=512):
    tg = min(tg, seq)
    ns = seq // tg
    qk_w, v_w = GLA_HEADS * GLA_DK, GLA_HEADS * GLA_DV
    rb = lambda col, w: pl.BlockSpec((tg, w), lambda b, i: (b * ns + i, col // w))
    full = lambda a: pl.BlockSpec(a.shape, lambda b, i: (0,) * a.ndim)
    return pl.pallas_call(
        _gla_body,
        grid=(bsz, ns),
        in_specs=[rb(COL_GLA_Q, qk_w), rb(COL_GLA_K, qk_w), rb(COL_GLA_V, v_w), rb(COL_GLA_G, v_w),
                  pl.BlockSpec((tg, LANES), lambda b, i: (b * ns + i, 0)),
                  full(wf_pad), full(bf_row), full(nw_row)],
        out_specs=pl.BlockSpec((tg, v_w), lambda b, i: (b * ns + i, 0)),
        out_shape=jax.ShapeDtypeStruct((bsz * seq, v_w), BF16),
        scratch_shapes=[pltpu.VMEM((GLA_HEADS, GLA_DV, GLA_DK), F32), pltpu.VMEM((tg, qk_w), F32)],
        compiler_params=_params(2),
        name="gla",
    )(proj, proj, proj, proj, small, wf_pad, bf_row, nw_row)


def _fox_body(q_ref, k_ref, v_ref, c_ref, bound_ref, o_ref, sa_ref, sb_ref, m_ref, l_ref, acc_ref, *, tq):
    qi = pl.program_id(2)
    q = q_ref[...]
    c0 = c_ref[pl.ds(qi, 1), :][:, 0:1]
    m_ref[...] = jnp.full_like(m_ref, -jnp.inf)
    l_ref[...] = jnp.zeros_like(l_ref)
    acc_ref[...] = jnp.zeros_like(acc_ref)
    n_lane_tiles = tq // LANES

    def key_rows(j):
        return pl.ds(pl.multiple_of(j * tq, tq), tq)

    def scores(j, s_ref):
        s_ref[...] = lax.dot_general(q, k_ref[key_rows(j), :], _NT, preferred_element_type=F32)

    def update(j, s_ref, masked):
        s = s_ref[...] - (c_ref[pl.ds(j, 1), :] - c0) * LOG2_E
        if masked:
            keep = (lax.broadcasted_iota(jnp.int32, s.shape, 1) <= lax.broadcasted_iota(jnp.int32, s.shape, 0))
            s = jnp.where(keep, s, -jnp.inf)
        m_prev = m_ref[...]
        m_new = jnp.maximum(m_prev, jnp.max(s, axis=-1, keepdims=True))
        p = jnp.concatenate([jnp.exp2(s[:, t * LANES:(t + 1) * LANES] - m_new) for t in range(n_lane_tiles)], axis=1)
        alpha = jnp.exp2(m_prev - m_new)
        l_ref[...] = alpha * l_ref[...] + jnp.sum(p, axis=-1, keepdims=True)
        acc_ref[...] = alpha * acc_ref[...] + jnp.dot(p.astype(BF16), v_ref[key_rows(j), :],
                                                      preferred_element_type=F32)
        m_ref[...] = m_new

    scores(qi, sa_ref)
    scores(jnp.maximum(qi - 1, 0), sb_ref)
    update(qi, sa_ref, True)

    m_min = jnp.min(m_ref[...], axis=0, keepdims=True)[:, 0:1]
    lane = lax.broadcasted_iota(jnp.int32, (1, LANES), 1)
    negligible = (bound_ref[pl.ds(qi, 1), :] + FOX_SKIP_LOG2 < m_min) & (lane < qi)
    j0 = jnp.sum(negligible.astype(jnp.int32))
    n_tiles = qi - j0

    def pair(t, carry):
        j = qi - 1 - 2 * t
        scores(jnp.maximum(j - 1, 0), sa_ref)
        update(j, sb_ref, False)
        scores(jnp.maximum(j - 2, 0), sb_ref)
        update(j - 1, sa_ref, False)
        return carry

    lax.fori_loop(0, n_tiles // 2, pair, 0)

    @pl.when(n_tiles % 2 == 1)
    def _():
        update(j0, sb_ref, False)

    o_ref[...] = (acc_ref[...] / l_ref[...]).astype(BF16)


def _fox_norm_body(q_ref, k_ref, qn_ref, kn_ref):
    lane = lax.broadcasted_iota(jnp.int32, (1, LANES), 1)

    def max_row_norms(ref):
        out = jnp.zeros((1, LANES), F32)
        for h in range(FOX_HEADS):
            x = ref[:, h * FOX_HD:(h + 1) * FOX_HD].astype(F32)
            sq = jnp.max(jnp.sum(x * x, axis=-1, keepdims=True), axis=0, keepdims=True)
            out = jnp.where(lane == h, jnp.sqrt(sq), out)
        return out

    qn_ref[...] = max_row_norms(q_ref)
    kn_ref[...] = max_row_norms(k_ref)


def fox_tile_norms(proj, n_tiles, tq):
    w = FOX_HEADS * FOX_HD
    out = jax.ShapeDtypeStruct((n_tiles, 1, LANES), F32)
    return pl.pallas_call(
        _fox_norm_body,
        grid=(n_tiles,),
        in_specs=[pl.BlockSpec((tq, w), lambda i: (i, COL_FOX_Q // w)),
                  pl.BlockSpec((tq, w), lambda i: (i, COL_FOX_K // w))],
        out_specs=[pl.BlockSpec((None, 1, LANES), lambda i: (i, 0, 0))] * 2,
        out_shape=[out, out],
        compiler_params=_params(1),
        name="fox_tile_norms",
    )(proj, proj)


def forgetting_attention(proj, c_blocks, bsz, seq, *, tq=512):
    tq = min(tq, seq)
    nq = seq // tq
    hd = FOX_HD
    qn, kn = fox_tile_norms(proj, bsz * nq, tq)
    per_head = lambda a: a[:, 0, :FOX_HEADS].reshape(bsz, nq, FOX_HEADS).transpose(0, 2, 1)
    kn_run = lax.cummax(per_head(kn), axis=2)
    c_first, c_last = c_blocks[..., 0], c_blocks[..., -1]
    bound = (per_head(qn)[..., :, None] * kn_run[..., None, :]
             - (c_last[..., None, :] - c_first[..., :, None]) * LOG2_E)
    bound = jnp.pad(bound, ((0, 0), (0, 0), (0, 0), (0, LANES - nq)))
    return pl.pallas_call(
        functools.partial(_fox_body, tq=tq),
        grid=(bsz, FOX_HEADS, nq),
        in_specs=[pl.BlockSpec((tq, hd), lambda b, h, i: (b * nq + i, COL_FOX_Q // hd + h)),
                  pl.BlockSpec((seq, hd), lambda b, h, i: (b, COL_FOX_K // hd + h)),
                  pl.BlockSpec((seq, hd), lambda b, h, i: (b, COL_FOX_V // hd + h)),
                  pl.BlockSpec((None, None, nq, tq), lambda b, h, i: (b, h, 0, 0)),
                  pl.BlockSpec((None, None, nq, LANES), lambda b, h, i: (b, h, 0, 0))],
        out_specs=pl.BlockSpec((tq, hd), lambda b, h, i: (b * nq + i, h)),
        out_shape=jax.ShapeDtypeStruct((bsz * seq, FOX_HEADS * hd), BF16),
        scratch_shapes=[pltpu.VMEM((tq, tq), F32), pltpu.VMEM((tq, tq), F32),
                        pltpu.VMEM((tq, LANES), F32), pltpu.VMEM((tq, LANES), F32), pltpu.VMEM((tq, hd), F32)],
        compiler_params=_params(3),
        name="fox",
    )(proj, proj, proj, c_blocks, bound)


def _mix_out_body(yc_ref, yg_ref, yf_ref, wc_ref, wg_ref, wf_ref, r_ref, o_ref):
    acc = jnp.dot(yc_ref[...], wc_ref[...], preferred_element_type=F32)
    acc += jnp.dot(yg_ref[...], wg_ref[...], preferred_element_type=F32)
    acc += jnp.dot(yf_ref[...], wf_ref[...], preferred_element_type=F32)
    o_ref[...] = r_ref[...] + acc


def mixer_out_proj(y_conv, y_gla, y_fox, w_out, layer, res, *, tm=1024, tn=512):
    m = res.shape[0]
    n = w_out.shape[2]
    tm = min(tm, m)
    kc, kg, kf = y_conv.shape[1], y_gla.shape[1], y_fox.shape[1]
    return pl.pallas_call(
        _mix_out_body,
        grid=(m // tm, n // tn),
        in_specs=[pl.BlockSpec((tm, kc), lambda i, j: (i, 0)),
                  pl.BlockSpec((tm, kg), lambda i, j: (i, 0)),
                  pl.BlockSpec((tm, kf), lambda i, j: (i, 0)),
                  pl.BlockSpec((None, kc, tn), lambda i, j: (layer, 0, j)),
                  pl.BlockSpec((None, kg, tn), lambda i, j: (layer, kc // kg, j)),
                  pl.BlockSpec((None, kf, tn), lambda i, j: (layer, (kc + kg) // kf, j)),
                  pl.BlockSpec((tm, tn), lambda i, j: (i, j))],
        out_specs=pl.BlockSpec((tm, tn), lambda i, j: (i, j)),
        out_shape=jax.ShapeDtypeStruct((m, n), F32),
        compiler_params=_params(2),
        name="mixer_out_proj",
    )(y_conv, y_gla, y_fox, w_out, w_out, w_out, res)


def _cross_body(q_ref, kt_ref, v_ref, w_ref, r_ref, o_ref, att_ref):
    @pl.when(pl.program_id(1) == 0)
    def _():
        for h in range(CROSS_HEADS):
            hs = slice(h * CROSS_HD, (h + 1) * CROSS_HD)
            s = jnp.dot(q_ref[:, hs], kt_ref[hs, :], preferred_element_type=F32) * (CROSS_HD ** -0.5)
            p = jnp.exp(s - jnp.max(s, axis=-1, keepdims=True))
            p = p / jnp.sum(p, axis=-1, keepdims=True)
            att_ref[:, hs] = jnp.dot(p.astype(BF16), v_ref[:, hs], preferred_element_type=F32).astype(BF16)

    o_ref[...] = r_ref[...] + jnp.dot(att_ref[...], w_ref[...], preferred_element_type=F32)


def cross_attention_out(q, k_t, v, w_co, layer, res, seq, *, tm=1024, tn=512):
    m, d = q.shape
    mem_len = v.shape[1]
    tm = min(tm, seq)
    return pl.pallas_call(
        _cross_body,
        grid=(m // tm, d // tn),
        in_specs=[pl.BlockSpec((tm, d), lambda i, j: (i, 0)),
                  pl.BlockSpec((None, d, mem_len), lambda i, j: ((i * tm) // seq, 0, 0)),
                  pl.BlockSpec((None, mem_len, d), lambda i, j: ((i * tm) // seq, 0, 0)),
                  pl.BlockSpec((None, d, tn), lambda i, j: (layer, 0, j)),
                  pl.BlockSpec((tm, tn), lambda i, j: (i, j))],
        out_specs=pl.BlockSpec((tm, tn), lambda i, j: (i, j)),
        out_shape=jax.ShapeDtypeStruct((m, d), F32),
        scratch_shapes=[pltpu.VMEM((tm, d), BF16)],
        compiler_params=_params(2),
        name="cross_attention_out",
    )(q, k_t, v, w_co, res)


def _router_body(x_ref, nw_ref, whi_ref, wlo_ref, xn_ref, lg_ref):
    xn = _rms_scale(x_ref[...], nw_ref[...])
    xn_ref[...] = xn
    hi = xn.astype(BF16)
    lo = (xn - hi.astype(F32)).astype(BF16)
    lg_ref[...] = (jnp.dot(hi, whi_ref[...], preferred_element_type=F32)
                   + (jnp.dot(hi, wlo_ref[...], preferred_element_type=F32)
                      + jnp.dot(lo, whi_ref[...], preferred_element_type=F32)))


def moe_router(x, nw, w_hi, w_lo, *, tm=256):
    m, k = x.shape
    tm = min(tm, m)
    return pl.pallas_call(
        _router_body,
        grid=(m // tm,),
        in_specs=[pl.BlockSpec((tm, k), lambda i: (i, 0)),
                  pl.BlockSpec((1, k), lambda i: (0, 0)),
                  pl.BlockSpec((k, LANES), lambda i: (0, 0)),
                  pl.BlockSpec((k, LANES), lambda i: (0, 0))],
        out_specs=[pl.BlockSpec((tm, k), lambda i: (i, 0)),
                   pl.BlockSpec((tm, LANES), lambda i: (i, 0))],
        out_shape=[jax.ShapeDtypeStruct((m, k), F32), jax.ShapeDtypeStruct((m, LANES), F32)],
        compiler_params=_params(1),
        name="moe_router",
    )(x, nw.reshape(1, k), w_hi, w_lo)


def _row_copy(src_hbm, src_row, dst, dst_row, sem):
    return pltpu.make_async_copy(src_hbm.at[pl.ds(src_row, 1)], dst.at[pl.ds(dst_row, 1)], sem)


def _experts_body(be_ref, tok_ref, nused_ref, x_hbm, wg_ref, wu_ref, wd_ref, y_ref, xbuf, sem, *, rb):
    i = pl.program_id(0)
    n_used = nused_ref[0]

    def start_gather(blk, slot):
        def body(r, carry):
            _row_copy(x_hbm, tok_ref[blk * rb + r], xbuf.at[slot], r, sem.at[slot]).start()
            return carry
        lax.fori_loop(0, rb, body, 0, unroll=GATHER_UNROLL)

    def wait_gather(slot):
        for r in range(rb):
            _row_copy(x_hbm, 0, xbuf.at[slot], r, sem.at[slot]).wait()

    @pl.when((i == 0) & (n_used > 0))
    def _():
        start_gather(0, 0)

    @pl.when(i + 1 < n_used)
    def _():
        start_gather(i + 1, (i + 1) % 2)

    @pl.when(i < n_used)
    def _():
        slot = i % 2
        wait_gather(slot)
        x = xbuf[slot].astype(BF16)
        g = jnp.dot(x, wg_ref[...], preferred_element_type=F32)
        u = jnp.dot(x, wu_ref[...], preferred_element_type=F32)
        h = (g * jax.nn.sigmoid(g) * u).astype(BF16)
        y_ref[...] = jnp.dot(h, wd_ref[...], preferred_element_type=F32)

    @pl.when(i >= n_used)
    def _():
        y_ref[...] = jnp.zeros_like(y_ref)


def moe_experts(xn, row_tok, block_expert, n_used, w_gate, w_up, w_down, layer, *, rb):
    d = xn.shape[1]
    n_rows = row_tok.shape[0]
    de = w_gate.shape[3]
    n_blocks = n_rows // rb
    grid_spec = pltpu.PrefetchScalarGridSpec(
        num_scalar_prefetch=3,
        grid=(n_blocks,),
        in_specs=[pl.BlockSpec(memory_space=pl.ANY),
                  pl.BlockSpec((None, None, d, de), lambda i, be, tok, nu: (layer, be[i], 0, 0)),
                  pl.BlockSpec((None, None, d, de), lambda i, be, tok, nu: (layer, be[i], 0, 0)),
                  pl.BlockSpec((None, None, de, d), lambda i, be, tok, nu: (layer, be[i], 0, 0))],
        out_specs=pl.BlockSpec((rb, d), lambda i, be, tok, nu: (i, 0)),
        scratch_shapes=[pltpu.VMEM((2, rb, d), F32), pltpu.SemaphoreType.DMA((2,))],
    )
    return pl.pallas_call(
        functools.partial(_experts_body, rb=rb),
        grid_spec=grid_spec,
        out_shape=jax.ShapeDtypeStruct((n_rows, d), F32),
        compiler_params=_params(1),
        name="moe_experts",
    )(block_expert, row_tok, n_used, xn, w_gate, w_up, w_down)


def _combine_body(dest_ref, h_ref, g_ref, nw_ref, y_hbm, o_ref, buf, sem, *, tc, normalize):
    i = pl.program_id(0)

    def start_gather(tile, slot):
        def body(t, carry):
            for k in range(TOP_K):
                _row_copy(y_hbm, dest_ref[(tile * tc + t) * TOP_K + k], buf.at[slot, k], t, sem.at[slot]).start()
            return carry
        lax.fori_loop(0, tc, body, 0, unroll=GATHER_UNROLL // TOP_K)

    @pl.when(i == 0)
    def _():
        start_gather(0, 0)

    @pl.when(i + 1 < pl.num_programs(0))
    def _():
        start_gather(i + 1, (i + 1) % 2)

    slot = i % 2
    for t in range(tc):
        for k in range(TOP_K):
            _row_copy(y_hbm, 0, buf.at[slot, k], t, sem.at[slot]).wait()
    g = g_ref[...]
    out = h_ref[...] + (buf[slot, 0] * g[:, 0:1] + buf[slot, 1] * g[:, 1:2])
    o_ref[...] = _rms_scale(out, nw_ref[...]) if normalize else out


def moe_combine(h, y_rows, dest, gates, out_norm_w=None, *, tc=128):
    t, d = h.shape
    tc = min(tc, t)
    normalize = out_norm_w is not None
    nw = (out_norm_w if normalize else jnp.ones((d,), F32)).reshape(1, d)
    grid_spec = pltpu.PrefetchScalarGridSpec(
        num_scalar_prefetch=1,
        grid=(t // tc,),
        in_specs=[pl.BlockSpec((tc, d), lambda i, dest: (i, 0)),
                  pl.BlockSpec((tc, TOP_K), lambda i, dest: (i, 0)),
                  pl.BlockSpec((1, d), lambda i, dest: (0, 0)),
                  pl.BlockSpec(memory_space=pl.ANY)],
        out_specs=pl.BlockSpec((tc, d), lambda i, dest: (i, 0)),
        scratch_shapes=[pltpu.VMEM((2, TOP_K, tc, d), F32), pltpu.SemaphoreType.DMA((2,))],
    )
    return pl.pallas_call(
        functools.partial(_combine_body, tc=tc, normalize=normalize),
        grid_spec=grid_spec,
        out_shape=jax.ShapeDtypeStruct((t, d), F32),
        compiler_params=_params(1),
        name="moe_combine",
    )(dest, h, gates, nw, y_rows)


def _routing_tables(logits, b_group, b_router, rb):
    t = logits.shape[0]
    group_logits = logits[:, :N_GROUPS] + b_group
    group = jnp.argmax(group_logits, axis=-1)
    p_group = jnp.take_along_axis(jax.nn.softmax(group_logits, axis=-1), group[:, None], axis=-1)[:, 0]
    exp_logits = (logits[:, N_GROUPS:N_GROUPS + N_EXPERTS] + b_router).reshape(t, N_GROUPS, EXPERTS_PER_GROUP)
    in_group = jnp.take_along_axis(exp_logits, group[:, None, None], axis=1)[:, 0]
    top_p, top_e = lax.top_k(jax.nn.softmax(in_group, axis=-1), TOP_K)
    gate = (p_group[:, None] * top_p / jnp.sum(top_p, axis=-1, keepdims=True)).reshape(-1)
    eid = (group[:, None] * EXPERTS_PER_GROUP + top_e).reshape(-1).astype(jnp.int32)

    n_assign = t * TOP_K
    seg = min(512, n_assign)
    onehot = (eid[:, None] == jnp.arange(N_EXPERTS, dtype=jnp.int32)[None, :]).astype(F32)
    within = jnp.einsum("ij,bjk->bik", jnp.tril(jnp.ones((seg, seg), F32)),
                        onehot.reshape(n_assign // seg, seg, N_EXPERTS), preferred_element_type=F32)
    seg_total = within[:, -1, :]
    seg_start = jnp.cumsum(seg_total, axis=0) - seg_total
    running = (within + seg_start[:, None, :]).reshape(n_assign, N_EXPERTS)
    counts = (seg_start[-1] + seg_total[-1]).astype(jnp.int32)
    rank = jnp.sum(running * onehot, axis=1).astype(jnp.int32) - 1
    padded = (counts + rb - 1) // rb * rb
    padded_end = jnp.cumsum(padded)
    dest = ((padded_end - padded)[eid] + rank).astype(jnp.int32)
    n_rows = n_assign + N_EXPERTS * rb
    n_blocks = n_rows // rb
    row_tok = jnp.zeros((n_rows,), jnp.int32).at[dest].set(jnp.arange(n_assign, dtype=jnp.int32) // TOP_K)
    block_row0 = jnp.arange(n_blocks, dtype=jnp.int32) * rb
    block_expert = jnp.minimum(jnp.sum((padded_end[None, :] <= block_row0[:, None]).astype(jnp.int32), axis=1),
                               N_EXPERTS - 1)
    n_used = (padded_end[-1] // rb).astype(jnp.int32).reshape(1)
    return dest, row_tok, gate.reshape(t, TOP_K), block_expert, n_used


def _pad_lanes(a):
    return jnp.pad(a, ((0, 0), (0, LANES - a.shape[1])))


def _hybrid_mixer(h, bsz, seq, layer, norm_w, w_in_f32, w_in_bf, conv_w, gla_wf2, gla_bf, gla_norm_w, fox_bf, w_out_bf):
    glr0 = COL_GLA_G + GLA_HEADS * GLA_DV
    ff0 = w_in_f32.shape[2] - FOX_HEADS
    fq0 = glr0 + GLA_RANK
    w_b = w_in_bf[layer, :, fq0:ff0]
    w_small = _pad_lanes(jnp.concatenate([w_in_f32[layer, :, glr0:fq0], w_in_f32[layer, :, ff0:]], axis=1)).astype(BF16)
    xn, small = rms_cast(h, norm_w, w_small)
    proj = in_proj(xn, w_in_bf, layer, w_b)

    y_conv = gated_conv(proj, conv_w, bsz, seq)

    wf_pad = jnp.pad(gla_wf2, ((SMALL_LR, LANES - SMALL_LR - GLA_RANK), (0, 0))).astype(BF16)
    y_gla = gated_linear_attention(proj, small, wf_pad, gla_bf.reshape(1, -1), gla_norm_w.reshape(1, -1), bsz, seq)

    bias_row = jnp.pad(fox_bf, (SMALL_F, LANES - SMALL_F - FOX_HEADS)).reshape(1, LANES)
    c = fox_cumulative_gate(small, bias_row, bsz, seq)
    tq = min(512, seq)
    c_blocks = (c[:, SMALL_F:SMALL_F + FOX_HEADS].reshape(bsz, seq, FOX_HEADS)
                .transpose(0, 2, 1).reshape(bsz, FOX_HEADS, seq // tq, tq))
    y_fox = forgetting_attention(proj, c_blocks, bsz, seq, tq=tq)

    return mixer_out_proj(y_conv, y_gla, y_fox, w_out_bf, layer, h)


def _cross_attention(h, bsz, seq, layer, norm_w, mem_n, w_cq, w_ck, w_cv, w_co):
    mem_len = mem_n.shape[0] // bsz
    q = matmul(rms_cast(h, norm_w), w_cq, layer)
    k = matmul(mem_n, w_ck, layer)
    v = matmul(mem_n, w_cv, layer)
    k_t = k.reshape(bsz, mem_len, D_MODEL).transpose(0, 2, 1)
    return cross_attention_out(q, k_t, v.reshape(bsz, mem_len, D_MODEL), w_co, layer, h, seq)


def _moe(h, layer, norm_w, w_group, b_group, w_router, b_router, w_gate, w_up, w_down, *, out_norm_w=None, rb=256):
    w_r = _pad_lanes(jnp.concatenate([w_group, w_router], axis=1))
    w_hi = w_r.astype(BF16)
    w_lo = (w_r - w_hi.astype(F32)).astype(BF16)
    xn, logits = moe_router(h, norm_w, w_hi, w_lo)
    dest, row_tok, gates, block_expert, n_used = _routing_tables(logits, b_group, b_router, rb)
    y_rows = moe_experts(xn, row_tok, block_expert, n_used, w_gate, w_up, w_down, layer, rb=rb)
    return moe_combine(h, y_rows, dest, gates, out_norm_w)


def kernel(x, mem, norm_mix_w, w_in, conv_w, gla_wf2, gla_bf, gla_norm_w, fox_bf, w_out, norm_cross_w, mem_norm_w, w_cq, w_ck, w_cv, w_co, norm_ffn_w, w_group, b_group, w_router, b_router, w_expert_gate, w_expert_up, w_expert_down, final_norm_w):
    bsz, seq, d = x.shape
    h = x.reshape(bsz * seq, d)
    mem_n = rms_cast(mem.reshape(-1, d), mem_norm_w)
    n_layers = norm_mix_w.shape[0]
    w_in_bf, w_out_bf = w_in.astype(BF16), w_out.astype(BF16)
    w_cq_bf, w_co_bf = w_cq.astype(BF16), w_co.astype(BF16)
    w_eg_bf, w_eu_bf, w_ed_bf = (w.astype(BF16) for w in (w_expert_gate, w_expert_up, w_expert_down))
    for l in range(n_layers):
        h = _hybrid_mixer(h, bsz, seq, l, norm_mix_w[l], w_in, w_in_bf, conv_w[l], gla_wf2[l], gla_bf[l],
                          gla_norm_w[l], fox_bf[l], w_out_bf)
        h = _cross_attention(h, bsz, seq, l, norm_cross_w[l], mem_n, w_cq_bf, w_ck, w_cv, w_co_bf)
        last = l == n_layers - 1
        h = _moe(h, l, norm_ffn_w[l], w_group[l], b_group[l], w_router[l], b_router[l], w_eg_bf, w_eu_bf, w_ed_bf,
                 out_norm_w=final_norm_w if last else None)
    return h.reshape(bsz, seq, d)
```

```python
import functools

import jax
import jax.numpy as jnp
from jax import lax
from jax.experimental import pallas as pl
from jax.experimental.pallas import tpu as pltpu

F32 = jnp.float32
BF16 = jnp.bfloat16

D_MODEL = 4096
CONV_WIDTH = 1024
GLA_HEADS = 4
GLA_DK = 128
GLA_DV = 256
GLA_RANK = 16
GLA_TAU = 16.0
GLA_CHUNK = 64
FOX_HEADS = 16
FOX_HD = 128
CROSS_HEADS = 4
CROSS_HD = 1024
N_GROUPS = 4
EXPERTS_PER_GROUP = 8
N_EXPERTS = 32
TOP_K = 2
D_EXPERT = 512
RMS_EPS = 1e-6
LOG2_E = 1.4426950408889634
FOX_SKIP_LOG2 = 40.0

LANES = 128
BF16_SUBLANES = 16
VMEM_LIMIT = 52 * 1024 * 1024
GATHER_UNROLL = 8

COL_CONV_H, COL_CONV_B, COL_CONV_C = 0, 1024, 2048
COL_GLA_Q, COL_GLA_K, COL_GLA_V, COL_GLA_G = 3072, 3584, 4096, 5120
COL_FOX_Q, COL_FOX_K, COL_FOX_V = 6144, 8192, 10240
MAIN_COLS = 12288
W_IN_GLA_LR = COL_FOX_Q
W_IN_FOX_Q = W_IN_GLA_LR + GLA_RANK
SMALL_LR, SMALL_F = 0, 16


def _params(n_axes):
    return pltpu.CompilerParams(dimension_semantics=("arbitrary",) * n_axes, vmem_limit_bytes=VMEM_LIMIT)


def _log_sigmoid(x):
    return jnp.minimum(x, 0.0) - jnp.log1p(jnp.exp(-jnp.abs(x)))


def _rms_scale(x, nw):
    ms = jnp.mean(x * x, axis=-1, keepdims=True)
    return x * lax.rsqrt(ms + RMS_EPS) * nw


def _rms_cast_body(x_ref, nw_ref, o_ref):
    o_ref[...] = _rms_scale(x_ref[...], nw_ref[...]).astype(BF16)


def _rms_cast_small_body(x_ref, nw_ref, ws_ref, o_ref, os_ref):
    xn = _rms_scale(x_ref[...], nw_ref[...]).astype(BF16)
    o_ref[...] = xn
    os_ref[...] = jnp.dot(xn, ws_ref[...], preferred_element_type=F32)


def rms_cast(x, nw, w_small=None, *, tm=256):
    m, k = x.shape
    tm = min(tm, m)
    row = pl.BlockSpec((tm, k), lambda i: (i, 0))
    in_specs = [row, pl.BlockSpec((1, k), lambda i: (0, 0))]
    if w_small is None:
        return pl.pallas_call(
            _rms_cast_body, grid=(m // tm,), in_specs=in_specs, out_specs=row,
            out_shape=jax.ShapeDtypeStruct((m, k), BF16), compiler_params=_params(1), name="rms_cast",
        )(x, nw.reshape(1, k))
    return pl.pallas_call(
        _rms_cast_small_body, grid=(m // tm,),
        in_specs=in_specs + [pl.BlockSpec((k, LANES), lambda i: (0, 0))],
        out_specs=[row, pl.BlockSpec((tm, LANES), lambda i: (i, 0))],
        out_shape=[jax.ShapeDtypeStruct((m, k), BF16), jax.ShapeDtypeStruct((m, LANES), F32)],
        compiler_params=_params(1), name="rms_cast_small",
    )(x, nw.reshape(1, k), w_small)


def _matmul_body(x_ref, w_ref, o_ref):
    o_ref[...] = jnp.dot(x_ref[...], w_ref[...].astype(BF16), preferred_element_type=F32).astype(o_ref.dtype)


def matmul(x, w, layer, *, tm=1024, tn=1024):
    m, k = x.shape
    n = w.shape[2]
    tm, tn = min(tm, m), min(tn, n)
    if w.dtype != BF16:
        assert m == tm, "f32 weights are only read once: single row tile"
        tn = min(tn, 512)
    return pl.pallas_call(
        _matmul_body,
        grid=(m // tm, n // tn),
        in_specs=[pl.BlockSpec((tm, k), lambda i, j: (i, 0)),
                  pl.BlockSpec((None, k, tn), lambda i, j: (layer, 0, j))],
        out_specs=pl.BlockSpec((tm, tn), lambda i, j: (i, j)),
        out_shape=jax.ShapeDtypeStruct((m, n), BF16),
        compiler_params=_params(2),
        name="matmul",
    )(x, w)


def _in_proj_body(x_ref, wa_ref, wb_ref, o_ref, *, na, n_fox_q):
    j = pl.program_id(1)

    @pl.when(j < na)
    def _():
        o_ref[...] = jnp.dot(x_ref[...], wa_ref[...], preferred_element_type=F32).astype(o_ref.dtype)

    @pl.when((j >= na) & (j < na + n_fox_q))
    def _():
        q = jnp.dot(x_ref[...], wb_ref[...], preferred_element_type=F32) * (FOX_HD ** -0.5 * LOG2_E)
        o_ref[...] = q.astype(o_ref.dtype)

    @pl.when(j >= na + n_fox_q)
    def _():
        o_ref[...] = jnp.dot(x_ref[...], wb_ref[...], preferred_element_type=F32).astype(o_ref.dtype)


def in_proj(xn, w_a, w_b, layer, *, tm=1024, tn=512):
    m, k = xn.shape
    tm = min(tm, m)
    na, nb = w_a.shape[2] // tn, w_b.shape[2] // tn
    return pl.pallas_call(
        functools.partial(_in_proj_body, na=na, n_fox_q=FOX_HEADS * FOX_HD // tn),
        grid=(m // tm, na + nb),
        in_specs=[pl.BlockSpec((tm, k), lambda i, j: (i, 0)),
                  pl.BlockSpec((None, k, tn), lambda i, j: (layer, 0, jnp.minimum(j, na - 1))),
                  pl.BlockSpec((None, k, tn), lambda i, j: (layer, 0, jnp.maximum(j - na, 0)))],
        out_specs=pl.BlockSpec((tm, tn), lambda i, j: (i, j)),
        out_shape=jax.ShapeDtypeStruct((m, (na + nb) * tn), BF16),
        compiler_params=_params(2),
        name="in_proj",
    )(xn, w_a, w_b)


def _conv_body(h_ref, b_ref, c_ref, ph_ref, pc_ref, w_ref, o_ref):
    i = pl.program_id(1)
    u = c_ref[...].astype(F32) * h_ref[...].astype(F32)
    up = pc_ref[...].astype(F32) * ph_ref[...].astype(F32)
    up = jnp.where(i > 0, up, 0.0)
    last, last2 = up[BF16_SUBLANES - 1:BF16_SUBLANES], up[BF16_SUBLANES - 2:BF16_SUBLANES - 1]
    row = lax.broadcasted_iota(jnp.int32, u.shape, 0)
    u1 = jnp.where(row == 0, last, pltpu.roll(u, 1, 0))
    u2 = jnp.where(row == 0, last2, jnp.where(row == 1, last, pltpu.roll(u, 2, 0)))
    w = w_ref[...]
    z = w[0:1] * u2 + w[1:2] * u1 + w[2:3] * u
    o_ref[...] = (b_ref[...].astype(F32) * z).astype(BF16)


def gated_conv(proj, conv_w, bsz, seq, *, ts=512):
    ts = min(ts, seq)
    ns = seq // ts
    cw = CONV_WIDTH
    cb = lambda col: pl.BlockSpec((ts, cw), lambda b, i: (b * ns + i, col // cw))
    pb = lambda col: pl.BlockSpec(
        (BF16_SUBLANES, cw),
        lambda b, i: (jnp.maximum((b * seq + i * ts) // BF16_SUBLANES - 1, 0), col // cw))
    return pl.pallas_call(
        _conv_body,
        grid=(bsz, ns),
        in_specs=[cb(COL_CONV_H), cb(COL_CONV_B), cb(COL_CONV_C), pb(COL_CONV_H), pb(COL_CONV_C),
                  pl.BlockSpec(conv_w.shape, lambda b, i: (0, 0))],
        out_specs=pl.BlockSpec((ts, cw), lambda b, i: (b * ns + i, 0)),
        out_shape=jax.ShapeDtypeStruct((bsz * seq, cw), BF16),
        compiler_params=_params(2),
        name="gated_conv",
    )(proj, proj, proj, proj, proj, conv_w)


def _fox_c_body(x_ref, bias_ref, o_ref, carry_ref):
    @pl.when(pl.program_id(1) == 0)
    def _():
        carry_ref[...] = jnp.zeros_like(carry_ref)

    lf = _log_sigmoid(x_ref[...] + bias_ref[...])
    ts = lf.shape[0]
    row = lax.broadcasted_iota(jnp.int32, lf.shape, 0)
    k = 1
    while k < ts:
        lf = lf + jnp.where(row >= k, pltpu.roll(lf, k, 0), 0.0)
        k *= 2
    c = lf + carry_ref[...]
    o_ref[...] = c
    carry_ref[...] = c[ts - 1:ts, :]


def fox_cumulative_gate(small, bias_row, bsz, seq, *, ts=512):
    ts = min(ts, seq)
    ns = seq // ts
    return pl.pallas_call(
        _fox_c_body,
        grid=(bsz, ns),
        in_specs=[pl.BlockSpec((ts, LANES), lambda b, i: (b * ns + i, 0)),
                  pl.BlockSpec((1, LANES), lambda b, i: (0, 0))],
        out_specs=pl.BlockSpec((ts, LANES), lambda b, i: (b * ns + i, 0)),
        out_shape=jax.ShapeDtypeStruct((bsz * seq, LANES), F32),
        scratch_shapes=[pltpu.VMEM((1, LANES), F32)],
        compiler_params=_params(2),
        name="fox_cumgate",
    )(small, bias_row)


_NT = (((1,), (1,)), ((), ()))
_TN = (((0,), (0,)), ((), ()))


def _gla_body(q_ref, k_ref, v_ref, g_ref, lr_ref, wf_ref, bf_ref, nw_ref, o_ref, st_ref, b_ref):
    @pl.when(pl.program_id(1) == 0)
    def _():
        st_ref[...] = jnp.zeros_like(st_ref)

    z = jnp.dot(lr_ref[...].astype(BF16), wf_ref[...], preferred_element_type=F32) + bf_ref[...]
    la = _log_sigmoid(z) * (1.0 / GLA_TAU)
    rin = lax.broadcasted_iota(jnp.int32, la.shape, 0) & (GLA_CHUNK - 1)
    k = 1
    while k < GLA_CHUNK:
        la = la + jnp.where(rin >= k, pltpu.roll(la, k, 0), 0.0)
        k *= 2
    b_ref[...] = la

    tril = (lax.broadcasted_iota(jnp.int32, (GLA_CHUNK, GLA_CHUNK), 0)
            >= lax.broadcasted_iota(jnp.int32, (GLA_CHUNK, GLA_CHUNK), 1))

    def chunk(c, carry):
        r0 = pl.multiple_of(c * GLA_CHUNK, GLA_CHUNK)
        rows = pl.ds(r0, GLA_CHUNK)
        for h in range(GLA_HEADS):
            ks = slice(h * GLA_DK, (h + 1) * GLA_DK)
            vs = slice(h * GLA_DV, (h + 1) * GLA_DV)
            bh = b_ref[rows, ks]
            bl = bh[GLA_CHUNK - 1:GLA_CHUNK, :]
            qh = q_ref[rows, ks].astype(F32) * (GLA_DK ** -0.5)
            kh = k_ref[rows, ks].astype(F32)
            vh = v_ref[rows, vs]
            q_in = (qh * jnp.exp(bh)).astype(BF16)
            k_in = (kh * jnp.exp(-bh)).astype(BF16)
            k_out = (kh * jnp.exp(bl - bh)).astype(BF16)
            att = lax.dot_general(q_in, k_in, _NT, preferred_element_type=F32)
            att = jnp.where(tril, att, 0.0).astype(BF16)
            st = st_ref[h]
            o = (jnp.dot(att, vh, preferred_element_type=F32)
                 + lax.dot_general(q_in, st.astype(BF16), _NT, preferred_element_type=F32))
            kv = lax.dot_general(vh, k_out, _TN, preferred_element_type=F32)
            st_ref[h] = st * jnp.exp(bl) + kv
            on = _rms_scale(o, nw_ref[:, vs])
            gg = g_ref[rows, vs].astype(F32)
            o_ref[rows, vs] = (on * (gg * jax.nn.sigmoid(gg))).astype(BF16)
        return carry

    lax.fori_loop(0, b_ref.shape[0] // GLA_CHUNK, chunk, 0, unroll=2)


def gated_linear_attention(proj, small, wf_pad, bf_row, nw_row, bsz, seq, *, tg=512):
    tg = min(tg, seq)
    ns = seq // tg
    qk_w, v_w = GLA_HEADS * GLA_DK, GLA_HEADS * GLA_DV
    rb = lambda col, w: pl.BlockSpec((tg, w), lambda b, i: (b * ns + i, col // w))
    full = lambda a: pl.BlockSpec(a.shape, lambda b, i: (0,) * a.ndim)
    return pl.pallas_call(
        _gla_body,
        grid=(bsz, ns),
        in_specs=[rb(COL_GLA_Q, qk_w), rb(COL_GLA_K, qk_w), rb(COL_GLA_V, v_w), rb(COL_GLA_G, v_w),
                  pl.BlockSpec((tg, LANES), lambda b, i: (b * ns + i, 0)),
                  full(wf_pad), full(bf_row), full(nw_row)],
        out_specs=pl.BlockSpec((tg, v_w), lambda b, i: (b * ns + i, 0)),
        out_shape=jax.ShapeDtypeStruct((bsz * seq, v_w), BF16),
        scratch_shapes=[pltpu.VMEM((GLA_HEADS, GLA_DV, GLA_DK), F32), pltpu.VMEM((tg, qk_w), F32)],
        compiler_params=_params(2),
        name="gla",
    )(proj, proj, proj, proj, small, wf_pad, bf_row, nw_row)


def _fox_body(q_ref, k_ref, v_ref, c_ref, bound_ref, o_ref, sa_ref, sb_ref, m_ref, l_ref, acc_ref, *, tq):
    qi = pl.program_id(2)
    q = q_ref[...]
    c0 = c_ref[pl.ds(qi, 1), :][:, 0:1]
    m_ref[...] = jnp.full_like(m_ref, -jnp.inf)
    l_ref[...] = jnp.zeros_like(l_ref)
    acc_ref[...] = jnp.zeros_like(acc_ref)
    n_lane_tiles = tq // LANES

    def key_rows(j):
        return pl.ds(pl.multiple_of(j * tq, tq), tq)

    def scores(j, s_ref):
        s_ref[...] = lax.dot_general(q, k_ref[key_rows(j), :], _NT, preferred_element_type=F32)

    def update(j, s_ref, masked):
        s = s_ref[...] - (c_ref[pl.ds(j, 1), :] - c0) * LOG2_E
        if masked:
            keep = (lax.broadcasted_iota(jnp.int32, s.shape, 1) <= lax.broadcasted_iota(jnp.int32, s.shape, 0))
            s = jnp.where(keep, s, -jnp.inf)
        m_prev = m_ref[...]
        m_new = jnp.maximum(m_prev, jnp.max(s, axis=-1, keepdims=True))
        p = jnp.concatenate([jnp.exp2(s[:, t * LANES:(t + 1) * LANES] - m_new) for t in range(n_lane_tiles)], axis=1)
        alpha = jnp.exp2(m_prev - m_new)
        l_ref[...] = alpha * l_ref[...] + jnp.sum(p, axis=-1, keepdims=True)
        acc_ref[...] = alpha * acc_ref[...] + jnp.dot(p.astype(BF16), v_ref[key_rows(j), :],
                                                      preferred_element_type=F32)
        m_ref[...] = m_new

    scores(qi, sa_ref)
    scores(jnp.maximum(qi - 1, 0), sb_ref)
    update(qi, sa_ref, True)

    m_min = jnp.min(m_ref[...], axis=0, keepdims=True)[:, 0:1]
    lane = lax.broadcasted_iota(jnp.int32, (1, LANES), 1)
    negligible = (bound_ref[pl.ds(qi, 1), :] + FOX_SKIP_LOG2 < m_min) & (lane < qi)
    j0 = jnp.sum(negligible.astype(jnp.int32))
    n_tiles = qi - j0

    def pair(t, carry):
        j = qi - 1 - 2 * t
        scores(jnp.maximum(j - 1, 0), sa_ref)
        update(j, sb_ref, False)
        scores(jnp.maximum(j - 2, 0), sb_ref)
        update(j - 1, sa_ref, False)
        return carry

    lax.fori_loop(0, n_tiles // 2, pair, 0)

    @pl.when(n_tiles % 2 == 1)
    def _():
        update(j0, sb_ref, False)

    o_ref[...] = (acc_ref[...] / l_ref[...]).astype(BF16)


def _fox_norm_body(q_ref, k_ref, qn_ref, kn_ref):
    lane = lax.broadcasted_iota(jnp.int32, (1, LANES), 1)

    def max_row_norms(ref):
        out = jnp.zeros((1, LANES), F32)
        for h in range(FOX_HEADS):
            x = ref[:, h * FOX_HD:(h + 1) * FOX_HD].astype(F32)
            sq = jnp.max(jnp.sum(x * x, axis=-1, keepdims=True), axis=0, keepdims=True)
            out = jnp.where(lane == h, jnp.sqrt(sq), out)
        return out

    qn_ref[...] = max_row_norms(q_ref)
    kn_ref[...] = max_row_norms(k_ref)


def fox_tile_norms(proj, n_tiles, tq):
    w = FOX_HEADS * FOX_HD
    out = jax.ShapeDtypeStruct((n_tiles, 1, LANES), F32)
    return pl.pallas_call(
        _fox_norm_body,
        grid=(n_tiles,),
        in_specs=[pl.BlockSpec((tq, w), lambda i: (i, COL_FOX_Q // w)),
                  pl.BlockSpec((tq, w), lambda i: (i, COL_FOX_K // w))],
        out_specs=[pl.BlockSpec((None, 1, LANES), lambda i: (i, 0, 0))] * 2,
        out_shape=[out, out],
        compiler_params=_params(1),
        name="fox_tile_norms",
    )(proj, proj)


def forgetting_attention(proj, c_blocks, bsz, seq, *, tq=512):
    tq = min(tq, seq)
    nq = seq // tq
    hd = FOX_HD
    qn, kn = fox_tile_norms(proj, bsz * nq, tq)
    per_head = lambda a: a[:, 0, :FOX_HEADS].reshape(bsz, nq, FOX_HEADS).transpose(0, 2, 1)
    kn_run = lax.cummax(per_head(kn), axis=2)
    c_first, c_last = c_blocks[..., 0], c_blocks[..., -1]
    bound = (per_head(qn)[..., :, None] * kn_run[..., None, :]
             - (c_last[..., None, :] - c_first[..., :, None]) * LOG2_E)
    bound = jnp.pad(bound, ((0, 0), (0, 0), (0, 0), (0, LANES - nq)))
    return pl.pallas_call(
        functools.partial(_fox_body, tq=tq),
        grid=(bsz, FOX_HEADS, nq),
        in_specs=[pl.BlockSpec((tq, hd), lambda b, h, i: (b * nq + i, COL_FOX_Q // hd + h)),
                  pl.BlockSpec((seq, hd), lambda b, h, i: (b, COL_FOX_K // hd + h)),
                  pl.BlockSpec((seq, hd), lambda b, h, i: (b, COL_FOX_V // hd + h)),
                  pl.BlockSpec((None, None, nq, tq), lambda b, h, i: (b, h, 0, 0)),
                  pl.BlockSpec((None, None, nq, LANES), lambda b, h, i: (b, h, 0, 0))],
        out_specs=pl.BlockSpec((tq, hd), lambda b, h, i: (b * nq + i, h)),
        out_shape=jax.ShapeDtypeStruct((bsz * seq, FOX_HEADS * hd), BF16),
        scratch_shapes=[pltpu.VMEM((tq, tq), F32), pltpu.VMEM((tq, tq), F32),
                        pltpu.VMEM((tq, LANES), F32), pltpu.VMEM((tq, LANES), F32), pltpu.VMEM((tq, hd), F32)],
        compiler_params=_params(3),
        name="fox",
    )(proj, proj, proj, c_blocks, bound)


def _mix_out_body(yc_ref, yg_ref, yf_ref, wc_ref, wg_ref, wf_ref, r_ref, o_ref):
    acc = jnp.dot(yc_ref[...], wc_ref[...], preferred_element_type=F32)
    acc += jnp.dot(yg_ref[...], wg_ref[...], preferred_element_type=F32)
    acc += jnp.dot(yf_ref[...], wf_ref[...], preferred_element_type=F32)
    o_ref[...] = r_ref[...] + acc


def mixer_out_proj(y_conv, y_gla, y_fox, w_out, layer, res, *, tm=1024, tn=512):
    m = res.shape[0]
    n = w_out.shape[2]
    tm = min(tm, m)
    kc, kg, kf = y_conv.shape[1], y_gla.shape[1], y_fox.shape[1]
    return pl.pallas_call(
        _mix_out_body,
        grid=(m // tm, n // tn),
        in_specs=[pl.BlockSpec((tm, kc), lambda i, j: (i, 0)),
                  pl.BlockSpec((tm, kg), lambda i, j: (i, 0)),
                  pl.BlockSpec((tm, kf), lambda i, j: (i, 0)),
                  pl.BlockSpec((None, kc, tn), lambda i, j: (layer, 0, j)),
                  pl.BlockSpec((None, kg, tn), lambda i, j: (layer, kc // kg, j)),
                  pl.BlockSpec((None, kf, tn), lambda i, j: (layer, (kc + kg) // kf, j)),
                  pl.BlockSpec((tm, tn), lambda i, j: (i, j))],
        out_specs=pl.BlockSpec((tm, tn), lambda i, j: (i, j)),
        out_shape=jax.ShapeDtypeStruct((m, n), F32),
        compiler_params=_params(2),
        name="mixer_out_proj",
    )(y_conv, y_gla, y_fox, w_out, w_out, w_out, res)


def _cross_body(q_ref, kt_ref, v_ref, w_ref, r_ref, o_ref, att_ref):
    @pl.when(pl.program_id(1) == 0)
    def _():
        for h in range(CROSS_HEADS):
            hs = slice(h * CROSS_HD, (h + 1) * CROSS_HD)
            s = jnp.dot(q_ref[:, hs], kt_ref[hs, :], preferred_element_type=F32) * (CROSS_HD ** -0.5)
            p = jnp.exp(s - jnp.max(s, axis=-1, keepdims=True))
            p = p / jnp.sum(p, axis=-1, keepdims=True)
            att_ref[:, hs] = jnp.dot(p.astype(BF16), v_ref[:, hs], preferred_element_type=F32).astype(BF16)

    o_ref[...] = r_ref[...] + jnp.dot(att_ref[...], w_ref[...], preferred_element_type=F32)


def cross_attention_out(q, k_t, v, w_co, layer, res, seq, *, tm=1024, tn=512):
    m, d = q.shape
    mem_len = v.shape[1]
    tm = min(tm, seq)
    return pl.pallas_call(
        _cross_body,
        grid=(m // tm, d // tn),
        in_specs=[pl.BlockSpec((tm, d), lambda i, j: (i, 0)),
                  pl.BlockSpec((None, d, mem_len), lambda i, j: ((i * tm) // seq, 0, 0)),
                  pl.BlockSpec((None, mem_len, d), lambda i, j: ((i * tm) // seq, 0, 0)),
                  pl.BlockSpec((None, d, tn), lambda i, j: (layer, 0, j)),
                  pl.BlockSpec((tm, tn), lambda i, j: (i, j))],
        out_specs=pl.BlockSpec((tm, tn), lambda i, j: (i, j)),
        out_shape=jax.ShapeDtypeStruct((m, d), F32),
        scratch_shapes=[pltpu.VMEM((tm, d), BF16)],
        compiler_params=_params(2),
        name="cross_attention_out",
    )(q, k_t, v, w_co, res)


def _router_body(x_ref, nw_ref, whi_ref, wlo_ref, xn_ref, lg_ref):
    xn = _rms_scale(x_ref[...], nw_ref[...])
    xn_ref[...] = xn
    hi = xn.astype(BF16)
    lo = (xn - hi.astype(F32)).astype(BF16)
    lg_ref[...] = (jnp.dot(hi, whi_ref[...], preferred_element_type=F32)
                   + (jnp.dot(hi, wlo_ref[...], preferred_element_type=F32)
                      + jnp.dot(lo, whi_ref[...], preferred_element_type=F32)))


def moe_router(x, nw, w_hi, w_lo, *, tm=256):
    m, k = x.shape
    tm = min(tm, m)
    return pl.pallas_call(
        _router_body,
        grid=(m // tm,),
        in_specs=[pl.BlockSpec((tm, k), lambda i: (i, 0)),
                  pl.BlockSpec((1, k), lambda i: (0, 0)),
                  pl.BlockSpec((k, LANES), lambda i: (0, 0)),
                  pl.BlockSpec((k, LANES), lambda i: (0, 0))],
        out_specs=[pl.BlockSpec((tm, k), lambda i: (i, 0)),
                   pl.BlockSpec((tm, LANES), lambda i: (i, 0))],
        out_shape=[jax.ShapeDtypeStruct((m, k), F32), jax.ShapeDtypeStruct((m, LANES), F32)],
        compiler_params=_params(1),
        name="moe_router",
    )(x, nw.reshape(1, k), w_hi, w_lo)


def _row_copy(src_hbm, src_row, dst, dst_row, sem):
    return pltpu.make_async_copy(src_hbm.at[pl.ds(src_row, 1)], dst.at[pl.ds(dst_row, 1)], sem)


def _experts_body(be_ref, tok_ref, nused_ref, x_hbm, wg_ref, wu_ref, wd_ref, y_ref, xbuf, sem, *, rb):
    i = pl.program_id(0)
    n_used = nused_ref[0]

    def start_gather(blk, slot):
        def body(r, carry):
            _row_copy(x_hbm, tok_ref[blk * rb + r], xbuf.at[slot], r, sem.at[slot]).start()
            return carry
        lax.fori_loop(0, rb, body, 0, unroll=GATHER_UNROLL)

    def wait_gather(slot):
        for r in range(rb):
            _row_copy(x_hbm, 0, xbuf.at[slot], r, sem.at[slot]).wait()

    @pl.when((i == 0) & (n_used > 0))
    def _():
        start_gather(0, 0)

    @pl.when(i + 1 < n_used)
    def _():
        start_gather(i + 1, (i + 1) % 2)

    @pl.when(i < n_used)
    def _():
        slot = i % 2
        wait_gather(slot)
        x = xbuf[slot].astype(BF16)
        g = jnp.dot(x, wg_ref[...], preferred_element_type=F32)
        u = jnp.dot(x, wu_ref[...], preferred_element_type=F32)
        h = (g * jax.nn.sigmoid(g) * u).astype(BF16)
        y_ref[...] = jnp.dot(h, wd_ref[...], preferred_element_type=F32)

    @pl.when(i >= n_used)
    def _():
        y_ref[...] = jnp.zeros_like(y_ref)


def moe_experts(xn, row_tok, block_expert, n_used, w_gate, w_up, w_down, layer, *, rb):
    d = xn.shape[1]
    n_rows = row_tok.shape[0]
    de = w_gate.shape[3]
    n_blocks = n_rows // rb
    grid_spec = pltpu.PrefetchScalarGridSpec(
        num_scalar_prefetch=3,
        grid=(n_blocks,),
        in_specs=[pl.BlockSpec(memory_space=pl.ANY),
                  pl.BlockSpec((None, None, d, de), lambda i, be, tok, nu: (layer, be[i], 0, 0)),
                  pl.BlockSpec((None, None, d, de), lambda i, be, tok, nu: (layer, be[i], 0, 0)),
                  pl.BlockSpec((None, None, de, d), lambda i, be, tok, nu: (layer, be[i], 0, 0))],
        out_specs=pl.BlockSpec((rb, d), lambda i, be, tok, nu: (i, 0)),
        scratch_shapes=[pltpu.VMEM((2, rb, d), F32), pltpu.SemaphoreType.DMA((2,))],
    )
    return pl.pallas_call(
        functools.partial(_experts_body, rb=rb),
        grid_spec=grid_spec,
        out_shape=jax.ShapeDtypeStruct((n_rows, d), F32),
        compiler_params=_params(1),
        name="moe_experts",
    )(block_expert, row_tok, n_used, xn, w_gate, w_up, w_down)


def _combine_body(dest_ref, h_ref, g_ref, nw_ref, y_hbm, o_ref, buf, sem, *, tc, normalize):
    i = pl.program_id(0)

    def start_gather(tile, slot):
        def body(t, carry):
            for k in range(TOP_K):
                _row_copy(y_hbm, dest_ref[(tile * tc + t) * TOP_K + k], buf.at[slot, k], t, sem.at[slot]).start()
            return carry
        lax.fori_loop(0, tc, body, 0, unroll=GATHER_UNROLL // TOP_K)

    @pl.when(i == 0)
    def _():
        start_gather(0, 0)

    @pl.when(i + 1 < pl.num_programs(0))
    def _():
        start_gather(i + 1, (i + 1) % 2)

    slot = i % 2
    for t in range(tc):
        for k in range(TOP_K):
            _row_copy(y_hbm, 0, buf.at[slot, k], t, sem.at[slot]).wait()
    g = g_ref[...]
    out = h_ref[...] + (buf[slot, 0] * g[:, 0:1] + buf[slot, 1] * g[:, 1:2])
    o_ref[...] = _rms_scale(out, nw_ref[...]) if normalize else out


def moe_combine(h, y_rows, dest, gates, out_norm_w=None, *, tc=128):
    t, d = h.shape
    tc = min(tc, t)
    normalize = out_norm_w is not None
    nw = (out_norm_w if normalize else jnp.ones((d,), F32)).reshape(1, d)
    grid_spec = pltpu.PrefetchScalarGridSpec(
        num_scalar_prefetch=1,
        grid=(t // tc,),
        in_specs=[pl.BlockSpec((tc, d), lambda i, dest: (i, 0)),
                  pl.BlockSpec((tc, TOP_K), lambda i, dest: (i, 0)),
                  pl.BlockSpec((1, d), lambda i, dest: (0, 0)),
                  pl.BlockSpec(memory_space=pl.ANY)],
        out_specs=pl.BlockSpec((tc, d), lambda i, dest: (i, 0)),
        scratch_shapes=[pltpu.VMEM((2, TOP_K, tc, d), F32), pltpu.SemaphoreType.DMA((2,))],
    )
    return pl.pallas_call(
        functools.partial(_combine_body, tc=tc, normalize=normalize),
        grid_spec=grid_spec,
        out_shape=jax.ShapeDtypeStruct((t, d), F32),
        compiler_params=_params(1),
        name="moe_combine",
    )(dest, h, gates, nw, y_rows)


def _routing_tables(logits, b_group, b_router, rb):
    t = logits.shape[0]
    group_logits = logits[:, :N_GROUPS] + b_group
    group = jnp.argmax(group_logits, axis=-1)
    p_group = jnp.take_along_axis(jax.nn.softmax(group_logits, axis=-1), group[:, None], axis=-1)[:, 0]
    exp_logits = (logits[:, N_GROUPS:N_GROUPS + N_EXPERTS] + b_router).reshape(t, N_GROUPS, EXPERTS_PER_GROUP)
    in_group = jnp.take_along_axis(exp_logits, group[:, None, None], axis=1)[:, 0]
    top_p, top_e = lax.top_k(jax.nn.softmax(in_group, axis=-1), TOP_K)
    gate = (p_group[:, None] * top_p / jnp.sum(top_p, axis=-1, keepdims=True)).reshape(-1)
    eid = (group[:, None] * EXPERTS_PER_GROUP + top_e).reshape(-1).astype(jnp.int32)

    n_assign = t * TOP_K
    seg = min(512, n_assign)
    onehot = (eid[:, None] == jnp.arange(N_EXPERTS, dtype=jnp.int32)[None, :]).astype(F32)
    within = jnp.einsum("ij,bjk->bik", jnp.tril(jnp.ones((seg, seg), F32)),
                        onehot.reshape(n_assign // seg, seg, N_EXPERTS), preferred_element_type=F32)
    seg_total = within[:, -1, :]
    seg_start = jnp.cumsum(seg_total, axis=0) - seg_total
    running = (within + seg_start[:, None, :]).reshape(n_assign, N_EXPERTS)
    counts = (seg_start[-1] + seg_total[-1]).astype(jnp.int32)
    rank = jnp.sum(running * onehot, axis=1).astype(jnp.int32) - 1
    padded = (counts + rb - 1) // rb * rb
    padded_end = jnp.cumsum(padded)
    dest = ((padded_end - padded)[eid] + rank).astype(jnp.int32)
    n_rows = n_assign + N_EXPERTS * rb
    n_blocks = n_rows // rb
    row_tok = jnp.zeros((n_rows,), jnp.int32).at[dest].set(jnp.arange(n_assign, dtype=jnp.int32) // TOP_K)
    block_row0 = jnp.arange(n_blocks, dtype=jnp.int32) * rb
    block_expert = jnp.minimum(jnp.sum((padded_end[None, :] <= block_row0[:, None]).astype(jnp.int32), axis=1),
                               N_EXPERTS - 1)
    n_used = (padded_end[-1] // rb).astype(jnp.int32).reshape(1)
    return dest, row_tok, gate.reshape(t, TOP_K), block_expert, n_used


def _pad_lanes(a):
    return jnp.pad(a, ((0, 0), (0, LANES - a.shape[1])))


def _hybrid_mixer(h, bsz, seq, layer, norm_w, w_in, w_a_bf, w_b_bf, conv_w, gla_wf2, gla_bf, gla_norm_w, fox_bf, w_out_bf):
    ff0 = w_in.shape[2] - FOX_HEADS
    w_small = _pad_lanes(jnp.concatenate([w_in[layer, :, W_IN_GLA_LR:W_IN_FOX_Q], w_in[layer, :, ff0:]],
                                         axis=1)).astype(BF16)
    xn, small = rms_cast(h, norm_w, w_small)
    proj = in_proj(xn, w_a_bf, w_b_bf, layer)

    y_conv = gated_conv(proj, conv_w, bsz, seq)

    wf_pad = jnp.pad(gla_wf2, ((SMALL_LR, LANES - SMALL_LR - GLA_RANK), (0, 0))).astype(BF16)
    y_gla = gated_linear_attention(proj, small, wf_pad, gla_bf.reshape(1, -1), gla_norm_w.reshape(1, -1), bsz, seq)

    bias_row = jnp.pad(fox_bf, (SMALL_F, LANES - SMALL_F - FOX_HEADS)).reshape(1, LANES)
    c = fox_cumulative_gate(small, bias_row, bsz, seq)
    tq = min(512, seq)
    c_blocks = (c[:, SMALL_F:SMALL_F + FOX_HEADS].reshape(bsz, seq, FOX_HEADS)
                .transpose(0, 2, 1).reshape(bsz, FOX_HEADS, seq // tq, tq))
    y_fox = forgetting_attention(proj, c_blocks, bsz, seq, tq=tq)

    return mixer_out_proj(y_conv, y_gla, y_fox, w_out_bf, layer, h)


def _cross_attention(h, bsz, seq, layer, norm_w, mem_n, w_cq, w_ck, w_cv, w_co):
    mem_len = mem_n.shape[0] // bsz
    q = matmul(rms_cast(h, norm_w), w_cq, layer)
    k = matmul(mem_n, w_ck, layer)
    v = matmul(mem_n, w_cv, layer)
    k_t = k.reshape(bsz, mem_len, D_MODEL).transpose(0, 2, 1)
    return cross_attention_out(q, k_t, v.reshape(bsz, mem_len, D_MODEL), w_co, layer, h, seq)


def _moe(h, layer, norm_w, w_group, b_group, w_router, b_router, w_gate, w_up, w_down, *, out_norm_w=None, rb=256):
    w_r = _pad_lanes(jnp.concatenate([w_group, w_router], axis=1))
    w_hi = w_r.astype(BF16)
    w_lo = (w_r - w_hi.astype(F32)).astype(BF16)
    xn, logits = moe_router(h, norm_w, w_hi, w_lo)
    dest, row_tok, gates, block_expert, n_used = _routing_tables(logits, b_group, b_router, rb)
    y_rows = moe_experts(xn, row_tok, block_expert, n_used, w_gate, w_up, w_down, layer, rb=rb)
    return moe_combine(h, y_rows, dest, gates, out_norm_w)


def kernel(x, mem, norm_mix_w, w_in, conv_w, gla_wf2, gla_bf, gla_norm_w, fox_bf, w_out, norm_cross_w, mem_norm_w, w_cq, w_ck, w_cv, w_co, norm_ffn_w, w_group, b_group, w_router, b_router, w_expert_gate, w_expert_up, w_expert_down, final_norm_w):
    bsz, seq, d = x.shape
    h = x.reshape(bsz * seq, d)
    mem_n = rms_cast(mem.reshape(-1, d), mem_norm_w)
    n_layers = norm_mix_w.shape[0]
    w_a_bf = w_in[:, :, :W_IN_GLA_LR].astype(BF16)
    w_b_bf = w_in[:, :, W_IN_FOX_Q:w_in.shape[2] - FOX_HEADS].astype(BF16)
    w_out_bf = w_out.astype(BF16)
    w_cq_bf, w_co_bf = w_cq.astype(BF16), w_co.astype(BF16)
    w_eg_bf, w_eu_bf, w_ed_bf = (w.astype(BF16) for w in (w_expert_gate, w_expert_up, w_expert_down))
    for l in range(n_layers):
        h = _hybrid_mixer(h, bsz, seq, l, norm_mix_w[l], w_in, w_a_bf, w_b_bf, conv_w[l], gla_wf2[l], gla_bf[l],
                          gla_norm_w[l], fox_bf[l], w_out_bf)
        h = _cross_attention(h, bsz, seq, l, norm_cross_w[l], mem_n, w_cq_bf, w_ck, w_cv, w_co_bf)
        last = l == n_layers - 1
        h = _moe(h, l, norm_ffn_w[l], w_group[l], b_group[l], w_router[l], b_router[l], w_eg_bf, w_eu_bf, w_ed_bf,
                 out_norm_w=final_norm_w if last else None)
    return h.reshape(bsz, seq, d)
```

```python
import functools

import jax
import jax.numpy as jnp
from jax import lax
from jax.experimental import pallas as pl
from jax.experimental.pallas import tpu as pltpu

F32 = jnp.float32
BF16 = jnp.bfloat16

D_MODEL = 4096
CONV_WIDTH = 1024
GLA_HEADS = 4
GLA_DK = 128
GLA_DV = 256
GLA_RANK = 16
GLA_TAU = 16.0
GLA_CHUNK = 64
FOX_HEADS = 16
FOX_HD = 128
CROSS_HEADS = 4
CROSS_HD = 1024
N_GROUPS = 4
EXPERTS_PER_GROUP = 8
N_EXPERTS = 32
TOP_K = 2
D_EXPERT = 512
RMS_EPS = 1e-6
LOG2_E = 1.4426950408889634
FOX_SKIP_LOG2 = 40.0

LANES = 128
BF16_SUBLANES = 16
VMEM_LIMIT = 52 * 1024 * 1024
GATHER_UNROLL = 8

COL_CONV_H, COL_CONV_B, COL_CONV_C = 0, 1024, 2048
COL_GLA_Q, COL_GLA_K, COL_GLA_V, COL_GLA_G = 3072, 3584, 4096, 5120
COL_FOX_Q, COL_FOX_K, COL_FOX_V = 6144, 8192, 10240
MAIN_COLS = 12288
W_IN_GLA_LR = COL_FOX_Q
W_IN_FOX_Q = W_IN_GLA_LR + GLA_RANK
SMALL_LR, SMALL_F = 0, 16


def _params(n_axes):
    return pltpu.CompilerParams(dimension_semantics=("arbitrary",) * n_axes, vmem_limit_bytes=VMEM_LIMIT)


def _log_sigmoid(x):
    return jnp.minimum(x, 0.0) - jnp.log1p(jnp.exp(-jnp.abs(x)))


def _rms_scale(x, nw):
    ms = jnp.mean(x * x, axis=-1, keepdims=True)
    return x * lax.rsqrt(ms + RMS_EPS) * nw


def _rms_cast_body(x_ref, nw_ref, o_ref):
    o_ref[...] = _rms_scale(x_ref[...], nw_ref[...]).astype(BF16)


def _rms_cast_small_body(x_ref, nw_ref, ws_ref, o_ref, os_ref):
    xn = _rms_scale(x_ref[...], nw_ref[...]).astype(BF16)
    o_ref[...] = xn
    os_ref[...] = jnp.dot(xn, ws_ref[...], preferred_element_type=F32)


def rms_cast(x, nw, w_small=None, *, tm=256):
    m, k = x.shape
    tm = min(tm, m)
    row = pl.BlockSpec((tm, k), lambda i: (i, 0))
    in_specs = [row, pl.BlockSpec((1, k), lambda i: (0, 0))]
    if w_small is None:
        return pl.pallas_call(
            _rms_cast_body, grid=(m // tm,), in_specs=in_specs, out_specs=row,
            out_shape=jax.ShapeDtypeStruct((m, k), BF16), compiler_params=_params(1), name="rms_cast",
        )(x, nw.reshape(1, k))
    return pl.pallas_call(
        _rms_cast_small_body, grid=(m // tm,),
        in_specs=in_specs + [pl.BlockSpec((k, LANES), lambda i: (0, 0))],
        out_specs=[row, pl.BlockSpec((tm, LANES), lambda i: (i, 0))],
        out_shape=[jax.ShapeDtypeStruct((m, k), BF16), jax.ShapeDtypeStruct((m, LANES), F32)],
        compiler_params=_params(1), name="rms_cast_small",
    )(x, nw.reshape(1, k), w_small)


def _cast_slab_body(a_ref, b_ref, o_ref, *, shift):
    o_ref[...] = jnp.concatenate([a_ref[:, shift:], b_ref[:, :shift]], axis=1).astype(BF16)


def _cast_body(a_ref, o_ref):
    o_ref[...] = a_ref[...].astype(BF16)


def cast_slab(w, col0, width, *, tr=512, tc=512):
    n_layers, k, _ = w.shape
    base, shift = col0 // tc, col0 % tc
    assert shift < LANES and width % tc == 0 and k % tr == 0
    grid = (n_layers, k // tr, width // tc)
    out_spec = pl.BlockSpec((None, tr, tc), lambda l, i, j: (l, i, j))
    a_spec = pl.BlockSpec((None, tr, tc), lambda l, i, j: (l, i, base + j))
    out_shape = jax.ShapeDtypeStruct((n_layers, k, width), BF16)
    if shift == 0:
        return pl.pallas_call(_cast_body, grid=grid, in_specs=[a_spec], out_specs=out_spec, out_shape=out_shape,
                              compiler_params=_params(3), name="cast_slab")(w)
    b_spec = pl.BlockSpec((None, tr, LANES), lambda l, i, j: (l, i, (base + j + 1) * (tc // LANES)))
    return pl.pallas_call(functools.partial(_cast_slab_body, shift=shift), grid=grid, in_specs=[a_spec, b_spec],
                          out_specs=out_spec, out_shape=out_shape, compiler_params=_params(3),
                          name="cast_slab_shifted")(w, w)


def _matmul_body(x_ref, w_ref, o_ref):
    o_ref[...] = jnp.dot(x_ref[...], w_ref[...].astype(BF16), preferred_element_type=F32).astype(o_ref.dtype)


def matmul(x, w, layer, *, tm=1024, tn=1024):
    m, k = x.shape
    n = w.shape[2]
    tm, tn = min(tm, m), min(tn, n)
    if w.dtype != BF16:
        assert m == tm, "f32 weights are only read once: single row tile"
        tn = min(tn, 512)
    return pl.pallas_call(
        _matmul_body,
        grid=(m // tm, n // tn),
        in_specs=[pl.BlockSpec((tm, k), lambda i, j: (i, 0)),
                  pl.BlockSpec((None, k, tn), lambda i, j: (layer, 0, j))],
        out_specs=pl.BlockSpec((tm, tn), lambda i, j: (i, j)),
        out_shape=jax.ShapeDtypeStruct((m, n), BF16),
        compiler_params=_params(2),
        name="matmul",
    )(x, w)


def _in_proj_body(x_ref, wa_ref, wb_ref, o_ref, *, na, n_fox_q):
    j = pl.program_id(1)

    @pl.when(j < na)
    def _():
        o_ref[...] = jnp.dot(x_ref[...], wa_ref[...], preferred_element_type=F32).astype(o_ref.dtype)

    @pl.when((j >= na) & (j < na + n_fox_q))
    def _():
        q = jnp.dot(x_ref[...], wb_ref[...], preferred_element_type=F32) * (FOX_HD ** -0.5 * LOG2_E)
        o_ref[...] = q.astype(o_ref.dtype)

    @pl.when(j >= na + n_fox_q)
    def _():
        o_ref[...] = jnp.dot(x_ref[...], wb_ref[...], preferred_element_type=F32).astype(o_ref.dtype)


def in_proj(xn, w_a, w_b, layer, *, tm=1024, tn=512):
    m, k = xn.shape
    tm = min(tm, m)
    na, nb = w_a.shape[2] // tn, w_b.shape[2] // tn
    return pl.pallas_call(
        functools.partial(_in_proj_body, na=na, n_fox_q=FOX_HEADS * FOX_HD // tn),
        grid=(m // tm, na + nb),
        in_specs=[pl.BlockSpec((tm, k), lambda i, j: (i, 0)),
                  pl.BlockSpec((None, k, tn), lambda i, j: (layer, 0, jnp.minimum(j, na - 1))),
                  pl.BlockSpec((None, k, tn), lambda i, j: (layer, 0, jnp.maximum(j - na, 0)))],
        out_specs=pl.BlockSpec((tm, tn), lambda i, j: (i, j)),
        out_shape=jax.ShapeDtypeStruct((m, (na + nb) * tn), BF16),
        compiler_params=_params(2),
        name="in_proj",
    )(xn, w_a, w_b)


def _conv_body(h_ref, b_ref, c_ref, ph_ref, pc_ref, w_ref, o_ref):
    i = pl.program_id(1)
    u = c_ref[...].astype(F32) * h_ref[...].astype(F32)
    up = pc_ref[...].astype(F32) * ph_ref[...].astype(F32)
    up = jnp.where(i > 0, up, 0.0)
    last, last2 = up[BF16_SUBLANES - 1:BF16_SUBLANES], up[BF16_SUBLANES - 2:BF16_SUBLANES - 1]
    row = lax.broadcasted_iota(jnp.int32, u.shape, 0)
    u1 = jnp.where(row == 0, last, pltpu.roll(u, 1, 0))
    u2 = jnp.where(row == 0, last2, jnp.where(row == 1, last, pltpu.roll(u, 2, 0)))
    w = w_ref[...]
    z = w[0:1] * u2 + w[1:2] * u1 + w[2:3] * u
    o_ref[...] = (b_ref[...].astype(F32) * z).astype(BF16)


def gated_conv(proj, conv_w, bsz, seq, *, ts=512):
    ts = min(ts, seq)
    ns = seq // ts
    cw = CONV_WIDTH
    cb = lambda col: pl.BlockSpec((ts, cw), lambda b, i: (b * ns + i, col // cw))
    pb = lambda col: pl.BlockSpec(
        (BF16_SUBLANES, cw),
        lambda b, i: (jnp.maximum((b * seq + i * ts) // BF16_SUBLANES - 1, 0), col // cw))
    return pl.pallas_call(
        _conv_body,
        grid=(bsz, ns),
        in_specs=[cb(COL_CONV_H), cb(COL_CONV_B), cb(COL_CONV_C), pb(COL_CONV_H), pb(COL_CONV_C),
                  pl.BlockSpec(conv_w.shape, lambda b, i: (0, 0))],
        out_specs=pl.BlockSpec((ts, cw), lambda b, i: (b * ns + i, 0)),
        out_shape=jax.ShapeDtypeStruct((bsz * seq, cw), BF16),
        compiler_params=_params(2),
        name="gated_conv",
    )(proj, proj, proj, proj, proj, conv_w)


def _fox_c_body(x_ref, bias_ref, o_ref, carry_ref):
    @pl.when(pl.program_id(1) == 0)
    def _():
        carry_ref[...] = jnp.zeros_like(carry_ref)

    lf = _log_sigmoid(x_ref[...] + bias_ref[...])
    ts = lf.shape[0]
    row = lax.broadcasted_iota(jnp.int32, lf.shape, 0)
    k = 1
    while k < ts:
        lf = lf + jnp.where(row >= k, pltpu.roll(lf, k, 0), 0.0)
        k *= 2
    c = lf + carry_ref[...]
    o_ref[...] = c
    carry_ref[...] = c[ts - 1:ts, :]


def fox_cumulative_gate(small, bias_row, bsz, seq, *, ts=512):
    ts = min(ts, seq)
    ns = seq // ts
    return pl.pallas_call(
        _fox_c_body,
        grid=(bsz, ns),
        in_specs=[pl.BlockSpec((ts, LANES), lambda b, i: (b * ns + i, 0)),
                  pl.BlockSpec((1, LANES), lambda b, i: (0, 0))],
        out_specs=pl.BlockSpec((ts, LANES), lambda b, i: (b * ns + i, 0)),
        out_shape=jax.ShapeDtypeStruct((bsz * seq, LANES), F32),
        scratch_shapes=[pltpu.VMEM((1, LANES), F32)],
        compiler_params=_params(2),
        name="fox_cumgate",
    )(small, bias_row)


_NT = (((1,), (1,)), ((), ()))
_TN = (((0,), (0,)), ((), ()))


def _gla_body(q_ref, k_ref, v_ref, g_ref, lr_ref, wf_ref, bf_ref, nw_ref, o_ref, st_ref, b_ref):
    @pl.when(pl.program_id(1) == 0)
    def _():
        st_ref[...] = jnp.zeros_like(st_ref)

    z = jnp.dot(lr_ref[...].astype(BF16), wf_ref[...], preferred_element_type=F32) + bf_ref[...]
    la = _log_sigmoid(z) * (1.0 / GLA_TAU)
    rin = lax.broadcasted_iota(jnp.int32, la.shape, 0) & (GLA_CHUNK - 1)
    k = 1
    while k < GLA_CHUNK:
        la = la + jnp.where(rin >= k, pltpu.roll(la, k, 0), 0.0)
        k *= 2
    b_ref[...] = la

    tril = (lax.broadcasted_iota(jnp.int32, (GLA_CHUNK, GLA_CHUNK), 0)
            >= lax.broadcasted_iota(jnp.int32, (GLA_CHUNK, GLA_CHUNK), 1))

    def chunk(c, carry):
        r0 = pl.multiple_of(c * GLA_CHUNK, GLA_CHUNK)
        rows = pl.ds(r0, GLA_CHUNK)
        for h in range(GLA_HEADS):
            ks = slice(h * GLA_DK, (h + 1) * GLA_DK)
            vs = slice(h * GLA_DV, (h + 1) * GLA_DV)
            bh = b_ref[rows, ks]
            bl = bh[GLA_CHUNK - 1:GLA_CHUNK, :]
            qh = q_ref[rows, ks].astype(F32) * (GLA_DK ** -0.5)
            kh = k_ref[rows, ks].astype(F32)
            vh = v_ref[rows, vs]
            q_in = (qh * jnp.exp(bh)).astype(BF16)
            k_in = (kh * jnp.exp(-bh)).astype(BF16)
            k_out = (kh * jnp.exp(bl - bh)).astype(BF16)
            att = lax.dot_general(q_in, k_in, _NT, preferred_element_type=F32)
            att = jnp.where(tril, att, 0.0).astype(BF16)
            st = st_ref[h]
            o = (jnp.dot(att, vh, preferred_element_type=F32)
                 + lax.dot_general(q_in, st.astype(BF16), _NT, preferred_element_type=F32))
            kv = lax.dot_general(vh, k_out, _TN, preferred_element_type=F32)
            st_ref[h] = st * jnp.exp(bl) + kv
            on = _rms_scale(o, nw_ref[:, vs])
            gg = g_ref[rows, vs].astype(F32)
            o_ref[rows, vs] = (on * (gg * jax.nn.sigmoid(gg))).astype(BF16)
        return carry

    lax.fori_loop(0, b_ref.shape[0] // GLA_CHUNK, chunk, 0, unroll=2)


def gated_linear_attention(proj, small, wf_pad, bf_row, nw_row, bsz, seq, *, tg=512):
    tg = min(tg, seq)
    ns = seq // tg
    qk_w, v_w = GLA_HEADS * GLA_DK, GLA_HEADS * GLA_DV
    rb = lambda col, w: pl.BlockSpec((tg, w), lambda b, i: (b * ns + i, col // w))
    full = lambda a: pl.BlockSpec(a.shape, lambda b, i: (0,) * a.ndim)
    return pl.pallas_call(
        _gla_body,
        grid=(bsz, ns),
        in_specs=[rb(COL_GLA_Q, qk_w), rb(COL_GLA_K, qk_w), rb(COL_GLA_V, v_w), rb(COL_GLA_G, v_w),
                  pl.BlockSpec((tg, LANES), lambda b, i: (b * ns + i, 0)),
                  full(wf_pad), full(bf_row), full(nw_row)],
        out_specs=pl.BlockSpec((tg, v_w), lambda b, i: (b * ns + i, 0)),
        out_shape=jax.ShapeDtypeStruct((bsz * seq, v_w), BF16),
        scratch_shapes=[pltpu.VMEM((GLA_HEADS, GLA_DV, GLA_DK), F32), pltpu.VMEM((tg, qk_w), F32)],
        compiler_params=_params(2),
        name="gla",
    )(proj, proj, proj, proj, small, wf_pad, bf_row, nw_row)


def _fox_body(q_ref, k_ref, v_ref, c_ref, bound_ref, o_ref, sa_ref, sb_ref, m_ref, l_ref, acc_ref, *, tq):
    qi = pl.program_id(2)
    q = q_ref[...]
    c0 = c_ref[pl.ds(qi, 1), :][:, 0:1]
    m_ref[...] = jnp.full_like(m_ref, -jnp.inf)
    l_ref[...] = jnp.zeros_like(l_ref)
    acc_ref[...] = jnp.zeros_like(acc_ref)
    n_lane_tiles = tq // LANES

    def key_rows(j):
        return pl.ds(pl.multiple_of(j * tq, tq), tq)

    def scores(j, s_ref):
        s_ref[...] = lax.dot_general(q, k_ref[key_rows(j), :], _NT, preferred_element_type=F32)

    def update(j, s_ref, masked):
        s = s_ref[...] - (c_ref[pl.ds(j, 1), :] - c0) * LOG2_E
        if masked:
            keep = (lax.broadcasted_iota(jnp.int32, s.shape, 1) <= lax.broadcasted_iota(jnp.int32, s.shape, 0))
            s = jnp.where(keep, s, -jnp.inf)
        m_prev = m_ref[...]
        m_new = jnp.maximum(m_prev, jnp.max(s, axis=-1, keepdims=True))
        p = jnp.concatenate([jnp.exp2(s[:, t * LANES:(t + 1) * LANES] - m_new) for t in range(n_lane_tiles)], axis=1)
        alpha = jnp.exp2(m_prev - m_new)
        l_ref[...] = alpha * l_ref[...] + jnp.sum(p, axis=-1, keepdims=True)
        acc_ref[...] = alpha * acc_ref[...] + jnp.dot(p.astype(BF16), v_ref[key_rows(j), :],
                                                      preferred_element_type=F32)
        m_ref[...] = m_new

    scores(qi, sa_ref)
    scores(jnp.maximum(qi - 1, 0), sb_ref)
    update(qi, sa_ref, True)

    m_min = jnp.min(m_ref[...], axis=0, keepdims=True)[:, 0:1]
    lane = lax.broadcasted_iota(jnp.int32, (1, LANES), 1)
    negligible = (bound_ref[pl.ds(qi, 1), :] + FOX_SKIP_LOG2 < m_min) & (lane < qi)
    j0 = jnp.sum(negligible.astype(jnp.int32))
    n_tiles = qi - j0

    def pair(t, carry):
        j = qi - 1 - 2 * t
        scores(jnp.maximum(j - 1, 0), sa_ref)
        update(j, sb_ref, False)
        scores(jnp.maximum(j - 2, 0), sb_ref)
        update(j - 1, sa_ref, False)
        return carry

    lax.fori_loop(0, n_tiles // 2, pair, 0)

    @pl.when(n_tiles % 2 == 1)
    def _():
        update(j0, sb_ref, False)

    o_ref[...] = (acc_ref[...] / l_ref[...]).astype(BF16)


def _fox_norm_body(q_ref, k_ref, qn_ref, kn_ref):
    lane = lax.broadcasted_iota(jnp.int32, (1, LANES), 1)

    def max_row_norms(ref):
        out = jnp.zeros((1, LANES), F32)
        for h in range(FOX_HEADS):
            x = ref[:, h * FOX_HD:(h + 1) * FOX_HD].astype(F32)
            sq = jnp.max(jnp.sum(x * x, axis=-1, keepdims=True), axis=0, keepdims=True)
            out = jnp.where(lane == h, jnp.sqrt(sq), out)
        return out

    qn_ref[...] = max_row_norms(q_ref)
    kn_ref[...] = max_row_norms(k_ref)


def fox_tile_norms(proj, n_tiles, tq):
    w = FOX_HEADS * FOX_HD
    out = jax.ShapeDtypeStruct((n_tiles, 1, LANES), F32)
    return pl.pallas_call(
        _fox_norm_body,
        grid=(n_tiles,),
        in_specs=[pl.BlockSpec((tq, w), lambda i: (i, COL_FOX_Q // w)),
                  pl.BlockSpec((tq, w), lambda i: (i, COL_FOX_K // w))],
        out_specs=[pl.BlockSpec((None, 1, LANES), lambda i: (i, 0, 0))] * 2,
        out_shape=[out, out],
        compiler_params=_params(1),
        name="fox_tile_norms",
    )(proj, proj)


def forgetting_attention(proj, c_blocks, bsz, seq, *, tq=512):
    tq = min(tq, seq)
    nq = seq // tq
    hd = FOX_HD
    qn, kn = fox_tile_norms(proj, bsz * nq, tq)
    per_head = lambda a: a[:, 0, :FOX_HEADS].reshape(bsz, nq, FOX_HEADS).transpose(0, 2, 1)
    kn_run = lax.cummax(per_head(kn), axis=2)
    c_first, c_last = c_blocks[..., 0], c_blocks[..., -1]
    bound = (per_head(qn)[..., :, None] * kn_run[..., None, :]
             - (c_last[..., None, :] - c_first[..., :, None]) * LOG2_E)
    bound = jnp.pad(bound, ((0, 0), (0, 0), (0, 0), (0, LANES - nq)))
    return pl.pallas_call(
        functools.partial(_fox_body, tq=tq),
        grid=(bsz, FOX_HEADS, nq),
        in_specs=[pl.BlockSpec((tq, hd), lambda b, h, i: (b * nq + i, COL_FOX_Q // hd + h)),
                  pl.BlockSpec((seq, hd), lambda b, h, i: (b, COL_FOX_K // hd + h)),
                  pl.BlockSpec((seq, hd), lambda b, h, i: (b, COL_FOX_V // hd + h)),
                  pl.BlockSpec((None, None, nq, tq), lambda b, h, i: (b, h, 0, 0)),
                  pl.BlockSpec((None, None, nq, LANES), lambda b, h, i: (b, h, 0, 0))],
        out_specs=pl.BlockSpec((tq, hd), lambda b, h, i: (b * nq + i, h)),
        out_shape=jax.ShapeDtypeStruct((bsz * seq, FOX_HEADS * hd), BF16),
        scratch_shapes=[pltpu.VMEM((tq, tq), F32), pltpu.VMEM((tq, tq), F32),
                        pltpu.VMEM((tq, LANES), F32), pltpu.VMEM((tq, LANES), F32), pltpu.VMEM((tq, hd), F32)],
        compiler_params=_params(3),
        name="fox",
    )(proj, proj, proj, c_blocks, bound)


def _mix_out_body(yc_ref, yg_ref, yf_ref, wc_ref, wg_ref, wf_ref, r_ref, o_ref):
    acc = jnp.dot(yc_ref[...], wc_ref[...], preferred_element_type=F32)
    acc += jnp.dot(yg_ref[...], wg_ref[...], preferred_element_type=F32)
    acc += jnp.dot(yf_ref[...], wf_ref[...], preferred_element_type=F32)
    o_ref[...] = r_ref[...] + acc


def mixer_out_proj(y_conv, y_gla, y_fox, w_out, layer, res, *, tm=1024, tn=512):
    m = res.shape[0]
    n = w_out.shape[2]
    tm = min(tm, m)
    kc, kg, kf = y_conv.shape[1], y_gla.shape[1], y_fox.shape[1]
    return pl.pallas_call(
        _mix_out_body,
        grid=(m // tm, n // tn),
        in_specs=[pl.BlockSpec((tm, kc), lambda i, j: (i, 0)),
                  pl.BlockSpec((tm, kg), lambda i, j: (i, 0)),
                  pl.BlockSpec((tm, kf), lambda i, j: (i, 0)),
                  pl.BlockSpec((None, kc, tn), lambda i, j: (layer, 0, j)),
                  pl.BlockSpec((None, kg, tn), lambda i, j: (layer, kc // kg, j)),
                  pl.BlockSpec((None, kf, tn), lambda i, j: (layer, (kc + kg) // kf, j)),
                  pl.BlockSpec((tm, tn), lambda i, j: (i, j))],
        out_specs=pl.BlockSpec((tm, tn), lambda i, j: (i, j)),
        out_shape=jax.ShapeDtypeStruct((m, n), F32),
        compiler_params=_params(2),
        name="mixer_out_proj",
    )(y_conv, y_gla, y_fox, w_out, w_out, w_out, res)


def _cross_body(q_ref, kt_ref, v_ref, w_ref, r_ref, o_ref, att_ref):
    @pl.when(pl.program_id(1) == 0)
    def _():
        for h in range(CROSS_HEADS):
            hs = slice(h * CROSS_HD, (h + 1) * CROSS_HD)
            s = jnp.dot(q_ref[:, hs], kt_ref[hs, :], preferred_element_type=F32) * (CROSS_HD ** -0.5)
            p = jnp.exp(s - jnp.max(s, axis=-1, keepdims=True))
            p = p / jnp.sum(p, axis=-1, keepdims=True)
            att_ref[:, hs] = jnp.dot(p.astype(BF16), v_ref[:, hs], preferred_element_type=F32).astype(BF16)

    o_ref[...] = r_ref[...] + jnp.dot(att_ref[...], w_ref[...], preferred_element_type=F32)


def cross_attention_out(q, k_t, v, w_co, layer, res, seq, *, tm=1024, tn=512):
    m, d = q.shape
    mem_len = v.shape[1]
    tm = min(tm, seq)
    return pl.pallas_call(
        _cross_body,
        grid=(m // tm, d // tn),
        in_specs=[pl.BlockSpec((tm, d), lambda i, j: (i, 0)),
                  pl.BlockSpec((None, d, mem_len), lambda i, j: ((i * tm) // seq, 0, 0)),
                  pl.BlockSpec((None, mem_len, d), lambda i, j: ((i * tm) // seq, 0, 0)),
                  pl.BlockSpec((None, d, tn), lambda i, j: (layer, 0, j)),
                  pl.BlockSpec((tm, tn), lambda i, j: (i, j))],
        out_specs=pl.BlockSpec((tm, tn), lambda i, j: (i, j)),
        out_shape=jax.ShapeDtypeStruct((m, d), F32),
        scratch_shapes=[pltpu.VMEM((tm, d), BF16)],
        compiler_params=_params(2),
        name="cross_attention_out",
    )(q, k_t, v, w_co, res)


def _router_body(x_ref, nw_ref, whi_ref, wlo_ref, xn_ref, lg_ref):
    xn = _rms_scale(x_ref[...], nw_ref[...])
    xn_ref[...] = xn
    hi = xn.astype(BF16)
    lo = (xn - hi.astype(F32)).astype(BF16)
    lg_ref[...] = (jnp.dot(hi, whi_ref[...], preferred_element_type=F32)
                   + (jnp.dot(hi, wlo_ref[...], preferred_element_type=F32)
                      + jnp.dot(lo, whi_ref[...], preferred_element_type=F32)))


def moe_router(x, nw, w_hi, w_lo, *, tm=256):
    m, k = x.shape
    tm = min(tm, m)
    return pl.pallas_call(
        _router_body,
        grid=(m // tm,),
        in_specs=[pl.BlockSpec((tm, k), lambda i: (i, 0)),
                  pl.BlockSpec((1, k), lambda i: (0, 0)),
                  pl.BlockSpec((k, LANES), lambda i: (0, 0)),
                  pl.BlockSpec((k, LANES), lambda i: (0, 0))],
        out_specs=[pl.BlockSpec((tm, k), lambda i: (i, 0)),
                   pl.BlockSpec((tm, LANES), lambda i: (i, 0))],
        out_shape=[jax.ShapeDtypeStruct((m, k), F32), jax.ShapeDtypeStruct((m, LANES), F32)],
        compiler_params=_params(1),
        name="moe_router",
    )(x, nw.reshape(1, k), w_hi, w_lo)


def _row_copy(src_hbm, src_row, dst, dst_row, sem):
    return pltpu.make_async_copy(src_hbm.at[pl.ds(src_row, 1)], dst.at[pl.ds(dst_row, 1)], sem)


def _experts_body(be_ref, tok_ref, nused_ref, x_hbm, wg_ref, wu_ref, wd_ref, y_ref, xbuf, sem, *, rb):
    i = pl.program_id(0)
    n_used = nused_ref[0]

    def start_gather(blk, slot):
        def body(r, carry):
            _row_copy(x_hbm, tok_ref[blk * rb + r], xbuf.at[slot], r, sem.at[slot]).start()
            return carry
        lax.fori_loop(0, rb, body, 0, unroll=GATHER_UNROLL)

    def wait_gather(slot):
        for r in range(rb):
            _row_copy(x_hbm, 0, xbuf.at[slot], r, sem.at[slot]).wait()

    @pl.when((i == 0) & (n_used > 0))
    def _():
        start_gather(0, 0)

    @pl.when(i + 1 < n_used)
    def _():
        start_gather(i + 1, (i + 1) % 2)

    @pl.when(i < n_used)
    def _():
        slot = i % 2
        wait_gather(slot)
        x = xbuf[slot].astype(BF16)
        g = jnp.dot(x, wg_ref[...], preferred_element_type=F32)
        u = jnp.dot(x, wu_ref[...], preferred_element_type=F32)
        h = (g * jax.nn.sigmoid(g) * u).astype(BF16)
        y_ref[...] = jnp.dot(h, wd_ref[...], preferred_element_type=F32)

    @pl.when(i >= n_used)
    def _():
        y_ref[...] = jnp.zeros_like(y_ref)


def moe_experts(xn, row_tok, block_expert, n_used, w_gate, w_up, w_down, layer, *, rb):
    d = xn.shape[1]
    n_rows = row_tok.shape[0]
    de = w_gate.shape[3]
    n_blocks = n_rows // rb
    grid_spec = pltpu.PrefetchScalarGridSpec(
        num_scalar_prefetch=3,
        grid=(n_blocks,),
        in_specs=[pl.BlockSpec(memory_space=pl.ANY),
                  pl.BlockSpec((None, None, d, de), lambda i, be, tok, nu: (layer, be[i], 0, 0)),
                  pl.BlockSpec((None, None, d, de), lambda i, be, tok, nu: (layer, be[i], 0, 0)),
                  pl.BlockSpec((None, None, de, d), lambda i, be, tok, nu: (layer, be[i], 0, 0))],
        out_specs=pl.BlockSpec((rb, d), lambda i, be, tok, nu: (i, 0)),
        scratch_shapes=[pltpu.VMEM((2, rb, d), F32), pltpu.SemaphoreType.DMA((2,))],
    )
    return pl.pallas_call(
        functools.partial(_experts_body, rb=rb),
        grid_spec=grid_spec,
        out_shape=jax.ShapeDtypeStruct((n_rows, d), F32),
        compiler_params=_params(1),
        name="moe_experts",
    )(block_expert, row_tok, n_used, xn, w_gate, w_up, w_down)


def _combine_body(dest_ref, h_ref, g_ref, nw_ref, y_hbm, o_ref, buf, sem, *, tc, normalize):
    i = pl.program_id(0)

    def start_gather(tile, slot):
        def body(t, carry):
            for k in range(TOP_K):
                _row_copy(y_hbm, dest_ref[(tile * tc + t) * TOP_K + k], buf.at[slot, k], t, sem.at[slot]).start()
            return carry
        lax.fori_loop(0, tc, body, 0, unroll=GATHER_UNROLL // TOP_K)

    @pl.when(i == 0)
    def _():
        start_gather(0, 0)

    @pl.when(i + 1 < pl.num_programs(0))
    def _():
        start_gather(i + 1, (i + 1) % 2)

    slot = i % 2
    for t in range(tc):
        for k in range(TOP_K):
            _row_copy(y_hbm, 0, buf.at[slot, k], t, sem.at[slot]).wait()
    g = g_ref[...]
    out = h_ref[...] + (buf[slot, 0] * g[:, 0:1] + buf[slot, 1] * g[:, 1:2])
    o_ref[...] = _rms_scale(out, nw_ref[...]) if normalize else out


def moe_combine(h, y_rows, dest, gates, out_norm_w=None, *, tc=128):
    t, d = h.shape
    tc = min(tc, t)
    normalize = out_norm_w is not None
    nw = (out_norm_w if normalize else jnp.ones((d,), F32)).reshape(1, d)
    grid_spec = pltpu.PrefetchScalarGridSpec(
        num_scalar_prefetch=1,
        grid=(t // tc,),
        in_specs=[pl.BlockSpec((tc, d), lambda i, dest: (i, 0)),
                  pl.BlockSpec((tc, TOP_K), lambda i, dest: (i, 0)),
                  pl.BlockSpec((1, d), lambda i, dest: (0, 0)),
                  pl.BlockSpec(memory_space=pl.ANY)],
        out_specs=pl.BlockSpec((tc, d), lambda i, dest: (i, 0)),
        scratch_shapes=[pltpu.VMEM((2, TOP_K, tc, d), F32), pltpu.SemaphoreType.DMA((2,))],
    )
    return pl.pallas_call(
        functools.partial(_combine_body, tc=tc, normalize=normalize),
        grid_spec=grid_spec,
        out_shape=jax.ShapeDtypeStruct((t, d), F32),
        compiler_params=_params(1),
        name="moe_combine",
    )(dest, h, gates, nw, y_rows)


def _routing_tables(logits, b_group, b_router, rb):
    t = logits.shape[0]
    group_logits = logits[:, :N_GROUPS] + b_group
    group = jnp.argmax(group_logits, axis=-1)
    p_group = jnp.take_along_axis(jax.nn.softmax(group_logits, axis=-1), group[:, None], axis=-1)[:, 0]
    exp_logits = (logits[:, N_GROUPS:N_GROUPS + N_EXPERTS] + b_router).reshape(t, N_GROUPS, EXPERTS_PER_GROUP)
    in_group = jnp.take_along_axis(exp_logits, group[:, None, None], axis=1)[:, 0]
    top_p, top_e = lax.top_k(jax.nn.softmax(in_group, axis=-1), TOP_K)
    gate = (p_group[:, None] * top_p / jnp.sum(top_p, axis=-1, keepdims=True)).reshape(-1)
    eid = (group[:, None] * EXPERTS_PER_GROUP + top_e).reshape(-1).astype(jnp.int32)

    n_assign = t * TOP_K
    seg = min(512, n_assign)
    onehot = (eid[:, None] == jnp.arange(N_EXPERTS, dtype=jnp.int32)[None, :]).astype(F32)
    within = jnp.einsum("ij,bjk->bik", jnp.tril(jnp.ones((seg, seg), F32)),
                        onehot.reshape(n_assign // seg, seg, N_EXPERTS), preferred_element_type=F32)
    seg_total = within[:, -1, :]
    seg_start = jnp.cumsum(seg_total, axis=0) - seg_total
    running = (within + seg_start[:, None, :]).reshape(n_assign, N_EXPERTS)
    counts = (seg_start[-1] + seg_total[-1]).astype(jnp.int32)
    rank = jnp.sum(running * onehot, axis=1).astype(jnp.int32) - 1
    padded = (counts + rb - 1) // rb * rb
    padded_end = jnp.cumsum(padded)
    dest = ((padded_end - padded)[eid] + rank).astype(jnp.int32)
    n_rows = n_assign + N_EXPERTS * rb
    n_blocks = n_rows // rb
    row_tok = jnp.zeros((n_rows,), jnp.int32).at[dest].set(jnp.arange(n_assign, dtype=jnp.int32) // TOP_K)
    block_row0 = jnp.arange(n_blocks, dtype=jnp.int32) * rb
    block_expert = jnp.minimum(jnp.sum((padded_end[None, :] <= block_row0[:, None]).astype(jnp.int32), axis=1),
                               N_EXPERTS - 1)
    n_used = (padded_end[-1] // rb).astype(jnp.int32).reshape(1)
    return dest, row_tok, gate.reshape(t, TOP_K), block_expert, n_used


def _pad_lanes(a):
    return jnp.pad(a, ((0, 0), (0, LANES - a.shape[1])))


def _hybrid_mixer(h, bsz, seq, layer, norm_w, w_in, w_a_bf, w_b_bf, conv_w, gla_wf2, gla_bf, gla_norm_w, fox_bf, w_out_bf):
    ff0 = w_in.shape[2] - FOX_HEADS
    w_small = _pad_lanes(jnp.concatenate([w_in[layer, :, W_IN_GLA_LR:W_IN_FOX_Q], w_in[layer, :, ff0:]],
                                         axis=1)).astype(BF16)
    xn, small = rms_cast(h, norm_w, w_small)
    proj = in_proj(xn, w_a_bf, w_b_bf, layer)

    y_conv = gated_conv(proj, conv_w, bsz, seq)

    wf_pad = jnp.pad(gla_wf2, ((SMALL_LR, LANES - SMALL_LR - GLA_RANK), (0, 0))).astype(BF16)
    y_gla = gated_linear_attention(proj, small, wf_pad, gla_bf.reshape(1, -1), gla_norm_w.reshape(1, -1), bsz, seq)

    bias_row = jnp.pad(fox_bf, (SMALL_F, LANES - SMALL_F - FOX_HEADS)).reshape(1, LANES)
    c = fox_cumulative_gate(small, bias_row, bsz, seq)
    tq = min(512, seq)
    c_blocks = (c[:, SMALL_F:SMALL_F + FOX_HEADS].reshape(bsz, seq, FOX_HEADS)
                .transpose(0, 2, 1).reshape(bsz, FOX_HEADS, seq // tq, tq))
    y_fox = forgetting_attention(proj, c_blocks, bsz, seq, tq=tq)

    return mixer_out_proj(y_conv, y_gla, y_fox, w_out_bf, layer, h)


def _cross_attention(h, bsz, seq, layer, norm_w, mem_n, w_cq, w_ck, w_cv, w_co):
    mem_len = mem_n.shape[0] // bsz
    q = matmul(rms_cast(h, norm_w), w_cq, layer)
    k = matmul(mem_n, w_ck, layer)
    v = matmul(mem_n, w_cv, layer)
    k_t = k.reshape(bsz, mem_len, D_MODEL).transpose(0, 2, 1)
    return cross_attention_out(q, k_t, v.reshape(bsz, mem_len, D_MODEL), w_co, layer, h, seq)


def _moe(h, layer, norm_w, w_group, b_group, w_router, b_router, w_gate, w_up, w_down, *, out_norm_w=None, rb=256):
    w_r = _pad_lanes(jnp.concatenate([w_group, w_router], axis=1))
    w_hi = w_r.astype(BF16)
    w_lo = (w_r - w_hi.astype(F32)).astype(BF16)
    xn, logits = moe_router(h, norm_w, w_hi, w_lo)
    dest, row_tok, gates, block_expert, n_used = _routing_tables(logits, b_group, b_router, rb)
    y_rows = moe_experts(xn, row_tok, block_expert, n_used, w_gate, w_up, w_down, layer, rb=rb)
    return moe_combine(h, y_rows, dest, gates, out_norm_w)


def kernel(x, mem, norm_mix_w, w_in, conv_w, gla_wf2, gla_bf, gla_norm_w, fox_bf, w_out, norm_cross_w, mem_norm_w, w_cq, w_ck, w_cv, w_co, norm_ffn_w, w_group, b_group, w_router, b_router, w_expert_gate, w_expert_up, w_expert_down, final_norm_w):
    bsz, seq, d = x.shape
    h = x.reshape(bsz * seq, d)
    mem_n = rms_cast(mem.reshape(-1, d), mem_norm_w)
    n_layers = norm_mix_w.shape[0]
    w_a_bf = cast_slab(w_in, 0, W_IN_GLA_LR)
    w_b_bf = cast_slab(w_in, W_IN_FOX_Q, 3 * FOX_HEADS * FOX_HD)
    w_out_bf = w_out.astype(BF16)
    w_cq_bf, w_co_bf = w_cq.astype(BF16), w_co.astype(BF16)
    w_eg_bf, w_eu_bf, w_ed_bf = (w.astype(BF16) for w in (w_expert_gate, w_expert_up, w_expert_down))
    for l in range(n_layers):
        h = _hybrid_mixer(h, bsz, seq, l, norm_mix_w[l], w_in, w_a_bf, w_b_bf, conv_w[l], gla_wf2[l], gla_bf[l],
                          gla_norm_w[l], fox_bf[l], w_out_bf)
        h = _cross_attention(h, bsz, seq, l, norm_cross_w[l], mem_n, w_cq_bf, w_ck, w_cv, w_co_bf)
        last = l == n_layers - 1
        h = _moe(h, l, norm_ffn_w[l], w_group[l], b_group[l], w_router[l], b_router[l], w_eg_bf, w_eu_bf, w_ed_bf,
                 out_norm_w=final_norm_w if last else None)
    return h.reshape(bsz, seq, d)
```

```python
import functools

import jax
import jax.numpy as jnp
from jax import lax
from jax.experimental import pallas as pl
from jax.experimental.pallas import tpu as pltpu

F32 = jnp.float32
BF16 = jnp.bfloat16

D_MODEL = 4096
CONV_WIDTH = 1024
GLA_HEADS = 4
GLA_DK = 128
GLA_DV = 256
GLA_RANK = 16
GLA_TAU = 16.0
GLA_CHUNK = 64
FOX_HEADS = 16
FOX_HD = 128
CROSS_HEADS = 4
CROSS_HD = 1024
N_GROUPS = 4
EXPERTS_PER_GROUP = 8
N_EXPERTS = 32
TOP_K = 2
D_EXPERT = 512
RMS_EPS = 1e-6
LOG2_E = 1.4426950408889634
FOX_SKIP_LOG2 = 40.0

LANES = 128
BF16_SUBLANES = 16
VMEM_LIMIT = 52 * 1024 * 1024
GATHER_UNROLL = 8

COL_CONV_H, COL_CONV_B, COL_CONV_C = 0, 1024, 2048
COL_GLA_Q, COL_GLA_K, COL_GLA_V, COL_GLA_G = 3072, 3584, 4096, 5120
COL_FOX_Q, COL_FOX_K, COL_FOX_V = 6144, 8192, 10240
MAIN_COLS = 12288
W_IN_GLA_LR = COL_FOX_Q
W_IN_FOX_Q = W_IN_GLA_LR + GLA_RANK
SMALL_LR, SMALL_F = 0, 16


def _params(n_axes):
    return pltpu.CompilerParams(dimension_semantics=("arbitrary",) * n_axes, vmem_limit_bytes=VMEM_LIMIT)


def _log_sigmoid(x):
    return jnp.minimum(x, 0.0) - jnp.log1p(jnp.exp(-jnp.abs(x)))


def _rms_scale(x, nw):
    ms = jnp.mean(x * x, axis=-1, keepdims=True)
    return x * lax.rsqrt(ms + RMS_EPS) * nw


def _rms_cast_body(x_ref, nw_ref, o_ref):
    o_ref[...] = _rms_scale(x_ref[...], nw_ref[...]).astype(BF16)


def _rms_cast_small_body(x_ref, nw_ref, ws_ref, o_ref, os_ref):
    xn = _rms_scale(x_ref[...], nw_ref[...]).astype(BF16)
    o_ref[...] = xn
    os_ref[...] = jnp.dot(xn, ws_ref[...], preferred_element_type=F32)


def rms_cast(x, nw, w_small=None, *, tm=256):
    m, k = x.shape
    tm = min(tm, m)
    row = pl.BlockSpec((tm, k), lambda i: (i, 0))
    in_specs = [row, pl.BlockSpec((1, k), lambda i: (0, 0))]
    if w_small is None:
        return pl.pallas_call(
            _rms_cast_body, grid=(m // tm,), in_specs=in_specs, out_specs=row,
            out_shape=jax.ShapeDtypeStruct((m, k), BF16), compiler_params=_params(1), name="rms_cast",
        )(x, nw.reshape(1, k))
    return pl.pallas_call(
        _rms_cast_small_body, grid=(m // tm,),
        in_specs=in_specs + [pl.BlockSpec((k, LANES), lambda i: (0, 0))],
        out_specs=[row, pl.BlockSpec((tm, LANES), lambda i: (i, 0))],
        out_shape=[jax.ShapeDtypeStruct((m, k), BF16), jax.ShapeDtypeStruct((m, LANES), F32)],
        compiler_params=_params(1), name="rms_cast_small",
    )(x, nw.reshape(1, k), w_small)


def _matmul_body(x_ref, w_ref, o_ref):
    o_ref[...] = jnp.dot(x_ref[...], w_ref[...].astype(BF16), preferred_element_type=F32).astype(o_ref.dtype)


def matmul(x, w, layer, *, tm=1024, tn=1024):
    m, k = x.shape
    n = w.shape[2]
    tm, tn = min(tm, m), min(tn, n)
    if w.dtype != BF16:
        assert m == tm, "f32 weights are only read once: single row tile"
        tn = min(tn, 512)
    return pl.pallas_call(
        _matmul_body,
        grid=(m // tm, n // tn),
        in_specs=[pl.BlockSpec((tm, k), lambda i, j: (i, 0)),
                  pl.BlockSpec((None, k, tn), lambda i, j: (layer, 0, j))],
        out_specs=pl.BlockSpec((tm, tn), lambda i, j: (i, j)),
        out_shape=jax.ShapeDtypeStruct((m, n), BF16),
        compiler_params=_params(2),
        name="matmul",
    )(x, w)


def _in_proj_body(x_ref, wa_ref, wb_ref, o_ref, *, na, n_fox_q):
    j = pl.program_id(1)

    @pl.when(j < na)
    def _():
        o_ref[...] = jnp.dot(x_ref[...], wa_ref[...], preferred_element_type=F32).astype(o_ref.dtype)

    @pl.when((j >= na) & (j < na + n_fox_q))
    def _():
        q = jnp.dot(x_ref[...], wb_ref[...], preferred_element_type=F32) * (FOX_HD ** -0.5 * LOG2_E)
        o_ref[...] = q.astype(o_ref.dtype)

    @pl.when(j >= na + n_fox_q)
    def _():
        o_ref[...] = jnp.dot(x_ref[...], wb_ref[...], preferred_element_type=F32).astype(o_ref.dtype)


def in_proj(xn, w_a, w_b, layer, *, tm=1024, tn=512):
    m, k = xn.shape
    tm = min(tm, m)
    na, nb = w_a.shape[2] // tn, w_b.shape[2] // tn
    return pl.pallas_call(
        functools.partial(_in_proj_body, na=na, n_fox_q=FOX_HEADS * FOX_HD // tn),
        grid=(m // tm, na + nb),
        in_specs=[pl.BlockSpec((tm, k), lambda i, j: (i, 0)),
                  pl.BlockSpec((None, k, tn), lambda i, j: (layer, 0, jnp.minimum(j, na - 1))),
                  pl.BlockSpec((None, k, tn), lambda i, j: (layer, 0, jnp.maximum(j - na, 0)))],
        out_specs=pl.BlockSpec((tm, tn), lambda i, j: (i, j)),
        out_shape=jax.ShapeDtypeStruct((m, (na + nb) * tn), BF16),
        compiler_params=_params(2),
        name="in_proj",
    )(xn, w_a, w_b)


def _conv_body(h_ref, b_ref, c_ref, ph_ref, pc_ref, w_ref, o_ref):
    i = pl.program_id(1)
    u = c_ref[...].astype(F32) * h_ref[...].astype(F32)
    up = pc_ref[...].astype(F32) * ph_ref[...].astype(F32)
    up = jnp.where(i > 0, up, 0.0)
    last, last2 = up[BF16_SUBLANES - 1:BF16_SUBLANES], up[BF16_SUBLANES - 2:BF16_SUBLANES - 1]
    row = lax.broadcasted_iota(jnp.int32, u.shape, 0)
    u1 = jnp.where(row == 0, last, pltpu.roll(u, 1, 0))
    u2 = jnp.where(row == 0, last2, jnp.where(row == 1, last, pltpu.roll(u, 2, 0)))
    w = w_ref[...]
    z = w[0:1] * u2 + w[1:2] * u1 + w[2:3] * u
    o_ref[...] = (b_ref[...].astype(F32) * z).astype(BF16)


def gated_conv(proj, conv_w, bsz, seq, *, ts=512):
    ts = min(ts, seq)
    ns = seq // ts
    cw = CONV_WIDTH
    cb = lambda col: pl.BlockSpec((ts, cw), lambda b, i: (b * ns + i, col // cw))
    pb = lambda col: pl.BlockSpec(
        (BF16_SUBLANES, cw),
        lambda b, i: (jnp.maximum((b * seq + i * ts) // BF16_SUBLANES - 1, 0), col // cw))
    return pl.pallas_call(
        _conv_body,
        grid=(bsz, ns),
        in_specs=[cb(COL_CONV_H), cb(COL_CONV_B), cb(COL_CONV_C), pb(COL_CONV_H), pb(COL_CONV_C),
                  pl.BlockSpec(conv_w.shape, lambda b, i: (0, 0))],
        out_specs=pl.BlockSpec((ts, cw), lambda b, i: (b * ns + i, 0)),
        out_shape=jax.ShapeDtypeStruct((bsz * seq, cw), BF16),
        compiler_params=_params(2),
        name="gated_conv",
    )(proj, proj, proj, proj, proj, conv_w)


def _fox_c_body(x_ref, bias_ref, o_ref, carry_ref):
    @pl.when(pl.program_id(1) == 0)
    def _():
        carry_ref[...] = jnp.zeros_like(carry_ref)

    lf = _log_sigmoid(x_ref[...] + bias_ref[...])
    ts = lf.shape[0]
    row = lax.broadcasted_iota(jnp.int32, lf.shape, 0)
    k = 1
    while k < ts:
        lf = lf + jnp.where(row >= k, pltpu.roll(lf, k, 0), 0.0)
        k *= 2
    c = lf + carry_ref[...]
    o_ref[...] = c
    carry_ref[...] = c[ts - 1:ts, :]


def fox_cumulative_gate(small, bias_row, bsz, seq, *, ts=512):
    ts = min(ts, seq)
    ns = seq // ts
    return pl.pallas_call(
        _fox_c_body,
        grid=(bsz, ns),
        in_specs=[pl.BlockSpec((ts, LANES), lambda b, i: (b * ns + i, 0)),
                  pl.BlockSpec((1, LANES), lambda b, i: (0, 0))],
        out_specs=pl.BlockSpec((ts, LANES), lambda b, i: (b * ns + i, 0)),
        out_shape=jax.ShapeDtypeStruct((bsz * seq, LANES), F32),
        scratch_shapes=[pltpu.VMEM((1, LANES), F32)],
        compiler_params=_params(2),
        name="fox_cumgate",
    )(small, bias_row)


_NT = (((1,), (1,)), ((), ()))
_TN = (((0,), (0,)), ((), ()))


def _gla_body(q_ref, k_ref, v_ref, g_ref, lr_ref, wf_ref, bf_ref, nw_ref, o_ref, st_ref, b_ref):
    @pl.when(pl.program_id(1) == 0)
    def _():
        st_ref[...] = jnp.zeros_like(st_ref)

    z = jnp.dot(lr_ref[...].astype(BF16), wf_ref[...], preferred_element_type=F32) + bf_ref[...]
    la = _log_sigmoid(z) * (1.0 / GLA_TAU)
    rin = lax.broadcasted_iota(jnp.int32, la.shape, 0) & (GLA_CHUNK - 1)
    k = 1
    while k < GLA_CHUNK:
        la = la + jnp.where(rin >= k, pltpu.roll(la, k, 0), 0.0)
        k *= 2
    b_ref[...] = la

    tril = (lax.broadcasted_iota(jnp.int32, (GLA_CHUNK, GLA_CHUNK), 0)
            >= lax.broadcasted_iota(jnp.int32, (GLA_CHUNK, GLA_CHUNK), 1))

    def chunk(c, carry):
        r0 = pl.multiple_of(c * GLA_CHUNK, GLA_CHUNK)
        rows = pl.ds(r0, GLA_CHUNK)
        for h in range(GLA_HEADS):
            ks = slice(h * GLA_DK, (h + 1) * GLA_DK)
            vs = slice(h * GLA_DV, (h + 1) * GLA_DV)
            bh = b_ref[rows, ks]
            bl = bh[GLA_CHUNK - 1:GLA_CHUNK, :]
            qh = q_ref[rows, ks].astype(F32) * (GLA_DK ** -0.5)
            kh = k_ref[rows, ks].astype(F32)
            vh = v_ref[rows, vs]
            q_in = (qh * jnp.exp(bh)).astype(BF16)
            k_in = (kh * jnp.exp(-bh)).astype(BF16)
            k_out = (kh * jnp.exp(bl - bh)).astype(BF16)
            att = lax.dot_general(q_in, k_in, _NT, preferred_element_type=F32)
            att = jnp.where(tril, att, 0.0).astype(BF16)
            st = st_ref[h]
            o = (jnp.dot(att, vh, preferred_element_type=F32)
                 + lax.dot_general(q_in, st.astype(BF16), _NT, preferred_element_type=F32))
            kv = lax.dot_general(vh, k_out, _TN, preferred_element_type=F32)
            st_ref[h] = st * jnp.exp(bl) + kv
            on = _rms_scale(o, nw_ref[:, vs])
            gg = g_ref[rows, vs].astype(F32)
            o_ref[rows, vs] = (on * (gg * jax.nn.sigmoid(gg))).astype(BF16)
        return carry

    lax.fori_loop(0, b_ref.shape[0] // GLA_CHUNK, chunk, 0, unroll=2)


def gated_linear_attention(proj, small, wf_pad, bf_row, nw_row, bsz, seq, *, tg=512):
    tg = min(tg, seq)
    ns = seq // tg
    qk_w, v_w = GLA_HEADS * GLA_DK, GLA_HEADS * GLA_DV
    rb = lambda col, w: pl.BlockSpec((tg, w), lambda b, i: (b * ns + i, col // w))
    full = lambda a: pl.BlockSpec(a.shape, lambda b, i: (0,) * a.ndim)
    return pl.pallas_call(
        _gla_body,
        grid=(bsz, ns),
        in_specs=[rb(COL_GLA_Q, qk_w), rb(COL_GLA_K, qk_w), rb(COL_GLA_V, v_w), rb(COL_GLA_G, v_w),
                  pl.BlockSpec((tg, LANES), lambda b, i: (b * ns + i, 0)),
                  full(wf_pad), full(bf_row), full(nw_row)],
        out_specs=pl.BlockSpec((tg, v_w), lambda b, i: (b * ns + i, 0)),
        out_shape=jax.ShapeDtypeStruct((bsz * seq, v_w), BF16),
        scratch_shapes=[pltpu.VMEM((GLA_HEADS, GLA_DV, GLA_DK), F32), pltpu.VMEM((tg, qk_w), F32)],
        compiler_params=_params(2),
        name="gla",
    )(proj, proj, proj, proj, small, wf_pad, bf_row, nw_row)


def _fox_body(q_ref, k_ref, v_ref, c_ref, bound_ref, o_ref, sa_ref, sb_ref, m_ref, l_ref, acc_ref, *, tq):
    qi = pl.program_id(2)
    q = q_ref[...]
    c0 = c_ref[pl.ds(qi, 1), :][:, 0:1]
    m_ref[...] = jnp.full_like(m_ref, -jnp.inf)
    l_ref[...] = jnp.zeros_like(l_ref)
    acc_ref[...] = jnp.zeros_like(acc_ref)
    n_lane_tiles = tq // LANES

    def key_rows(j):
        return pl.ds(pl.multiple_of(j * tq, tq), tq)

    def scores(j, s_ref):
        s_ref[...] = lax.dot_general(q, k_ref[key_rows(j), :], _NT, preferred_element_type=F32)

    def update(j, s_ref, masked):
        s = s_ref[...] - (c_ref[pl.ds(j, 1), :] - c0) * LOG2_E
        if masked:
            keep = (lax.broadcasted_iota(jnp.int32, s.shape, 1) <= lax.broadcasted_iota(jnp.int32, s.shape, 0))
            s = jnp.where(keep, s, -jnp.inf)
        m_prev = m_ref[...]
        m_new = jnp.maximum(m_prev, jnp.max(s, axis=-1, keepdims=True))
        p = jnp.concatenate([jnp.exp2(s[:, t * LANES:(t + 1) * LANES] - m_new) for t in range(n_lane_tiles)], axis=1)
        alpha = jnp.exp2(m_prev - m_new)
        l_ref[...] = alpha * l_ref[...] + jnp.sum(p, axis=-1, keepdims=True)
        acc_ref[...] = alpha * acc_ref[...] + jnp.dot(p.astype(BF16), v_ref[key_rows(j), :],
                                                      preferred_element_type=F32)
        m_ref[...] = m_new

    scores(qi, sa_ref)
    scores(jnp.maximum(qi - 1, 0), sb_ref)
    update(qi, sa_ref, True)

    m_min = jnp.min(m_ref[...], axis=0, keepdims=True)[:, 0:1]
    lane = lax.broadcasted_iota(jnp.int32, (1, LANES), 1)
    negligible = (bound_ref[pl.ds(qi, 1), :] + FOX_SKIP_LOG2 < m_min) & (lane < qi)
    j0 = jnp.sum(negligible.astype(jnp.int32))
    n_tiles = qi - j0

    def pair(t, carry):
        j = qi - 1 - 2 * t
        scores(jnp.maximum(j - 1, 0), sa_ref)
        update(j, sb_ref, False)
        scores(jnp.maximum(j - 2, 0), sb_ref)
        update(j - 1, sa_ref, False)
        return carry

    lax.fori_loop(0, n_tiles // 2, pair, 0)

    @pl.when(n_tiles % 2 == 1)
    def _():
        update(j0, sb_ref, False)

    o_ref[...] = (acc_ref[...] / l_ref[...]).astype(BF16)


def _fox_norm_body(q_ref, k_ref, qn_ref, kn_ref):
    lane = lax.broadcasted_iota(jnp.int32, (1, LANES), 1)

    def max_row_norms(ref):
        out = jnp.zeros((1, LANES), F32)
        for h in range(FOX_HEADS):
            x = ref[:, h * FOX_HD:(h + 1) * FOX_HD].astype(F32)
            sq = jnp.max(jnp.sum(x * x, axis=-1, keepdims=True), axis=0, keepdims=True)
            out = jnp.where(lane == h, jnp.sqrt(sq), out)
        return out

    qn_ref[...] = max_row_norms(q_ref)
    kn_ref[...] = max_row_norms(k_ref)


def fox_tile_norms(proj, n_tiles, tq):
    w = FOX_HEADS * FOX_HD
    out = jax.ShapeDtypeStruct((n_tiles, 1, LANES), F32)
    return pl.pallas_call(
        _fox_norm_body,
        grid=(n_tiles,),
        in_specs=[pl.BlockSpec((tq, w), lambda i: (i, COL_FOX_Q // w)),
                  pl.BlockSpec((tq, w), lambda i: (i, COL_FOX_K // w))],
        out_specs=[pl.BlockSpec((None, 1, LANES), lambda i: (i, 0, 0))] * 2,
        out_shape=[out, out],
        compiler_params=_params(1),
        name="fox_tile_norms",
    )(proj, proj)


def forgetting_attention(proj, c_blocks, bsz, seq, *, tq=512):
    tq = min(tq, seq)
    nq = seq // tq
    hd = FOX_HD
    qn, kn = fox_tile_norms(proj, bsz * nq, tq)
    per_head = lambda a: a[:, 0, :FOX_HEADS].reshape(bsz, nq, FOX_HEADS).transpose(0, 2, 1)
    kn_run = lax.cummax(per_head(kn), axis=2)
    c_first, c_last = c_blocks[..., 0], c_blocks[..., -1]
    bound = (per_head(qn)[..., :, None] * kn_run[..., None, :]
             - (c_last[..., None, :] - c_first[..., :, None]) * LOG2_E)
    bound = jnp.pad(bound, ((0, 0), (0, 0), (0, 0), (0, LANES - nq)))
    return pl.pallas_call(
        functools.partial(_fox_body, tq=tq),
        grid=(bsz, FOX_HEADS, nq),
        in_specs=[pl.BlockSpec((tq, hd), lambda b, h, i: (b * nq + i, COL_FOX_Q // hd + h)),
                  pl.BlockSpec((seq, hd), lambda b, h, i: (b, COL_FOX_K // hd + h)),
                  pl.BlockSpec((seq, hd), lambda b, h, i: (b, COL_FOX_V // hd + h)),
                  pl.BlockSpec((None, None, nq, tq), lambda b, h, i: (b, h, 0, 0)),
                  pl.BlockSpec((None, None, nq, LANES), lambda b, h, i: (b, h, 0, 0))],
        out_specs=pl.BlockSpec((tq, hd), lambda b, h, i: (b * nq + i, h)),
        out_shape=jax.ShapeDtypeStruct((bsz * seq, FOX_HEADS * hd), BF16),
        scratch_shapes=[pltpu.VMEM((tq, tq), F32), pltpu.VMEM((tq, tq), F32),
                        pltpu.VMEM((tq, LANES), F32), pltpu.VMEM((tq, LANES), F32), pltpu.VMEM((tq, hd), F32)],
        compiler_params=_params(3),
        name="fox",
    )(proj, proj, proj, c_blocks, bound)


def _mix_out_body(yc_ref, yg_ref, yf_ref, wc_ref, wg_ref, wf_ref, r_ref, o_ref):
    acc = jnp.dot(yc_ref[...], wc_ref[...], preferred_element_type=F32)
    acc += jnp.dot(yg_ref[...], wg_ref[...], preferred_element_type=F32)
    acc += jnp.dot(yf_ref[...], wf_ref[...], preferred_element_type=F32)
    o_ref[...] = r_ref[...] + acc


def mixer_out_proj(y_conv, y_gla, y_fox, w_out, layer, res, *, tm=1024, tn=512):
    m = res.shape[0]
    n = w_out.shape[2]
    tm = min(tm, m)
    kc, kg, kf = y_conv.shape[1], y_gla.shape[1], y_fox.shape[1]
    return pl.pallas_call(
        _mix_out_body,
        grid=(m // tm, n // tn),
        in_specs=[pl.BlockSpec((tm, kc), lambda i, j: (i, 0)),
                  pl.BlockSpec((tm, kg), lambda i, j: (i, 0)),
                  pl.BlockSpec((tm, kf), lambda i, j: (i, 0)),
                  pl.BlockSpec((None, kc, tn), lambda i, j: (layer, 0, j)),
                  pl.BlockSpec((None, kg, tn), lambda i, j: (layer, kc // kg, j)),
                  pl.BlockSpec((None, kf, tn), lambda i, j: (layer, (kc + kg) // kf, j)),
                  pl.BlockSpec((tm, tn), lambda i, j: (i, j))],
        out_specs=pl.BlockSpec((tm, tn), lambda i, j: (i, j)),
        out_shape=jax.ShapeDtypeStruct((m, n), F32),
        compiler_params=_params(2),
        name="mixer_out_proj",
    )(y_conv, y_gla, y_fox, w_out, w_out, w_out, res)


def _cross_body(q_ref, kt_ref, v_ref, w_ref, r_ref, o_ref, att_ref):
    @pl.when(pl.program_id(1) == 0)
    def _():
        for h in range(CROSS_HEADS):
            hs = slice(h * CROSS_HD, (h + 1) * CROSS_HD)
            s = jnp.dot(q_ref[:, hs], kt_ref[hs, :], preferred_element_type=F32) * (CROSS_HD ** -0.5)
            p = jnp.exp(s - jnp.max(s, axis=-1, keepdims=True))
            p = p / jnp.sum(p, axis=-1, keepdims=True)
            att_ref[:, hs] = jnp.dot(p.astype(BF16), v_ref[:, hs], preferred_element_type=F32).astype(BF16)

    o_ref[...] = r_ref[...] + jnp.dot(att_ref[...], w_ref[...], preferred_element_type=F32)


def cross_attention_out(q, k_t, v, w_co, layer, res, seq, *, tm=1024, tn=512):
    m, d = q.shape
    mem_len = v.shape[1]
    tm = min(tm, seq)
    return pl.pallas_call(
        _cross_body,
        grid=(m // tm, d // tn),
        in_specs=[pl.BlockSpec((tm, d), lambda i, j: (i, 0)),
                  pl.BlockSpec((None, d, mem_len), lambda i, j: ((i * tm) // seq, 0, 0)),
                  pl.BlockSpec((None, mem_len, d), lambda i, j: ((i * tm) // seq, 0, 0)),
                  pl.BlockSpec((None, d, tn), lambda i, j: (layer, 0, j)),
                  pl.BlockSpec((tm, tn), lambda i, j: (i, j))],
        out_specs=pl.BlockSpec((tm, tn), lambda i, j: (i, j)),
        out_shape=jax.ShapeDtypeStruct((m, d), F32),
        scratch_shapes=[pltpu.VMEM((tm, d), BF16)],
        compiler_params=_params(2),
        name="cross_attention_out",
    )(q, k_t, v, w_co, res)


def _router_body(x_ref, nw_ref, whi_ref, wlo_ref, xn_ref, lg_ref):
    xn = _rms_scale(x_ref[...], nw_ref[...])
    xn_ref[...] = xn
    hi = xn.astype(BF16)
    lo = (xn - hi.astype(F32)).astype(BF16)
    lg_ref[...] = (jnp.dot(hi, whi_ref[...], preferred_element_type=F32)
                   + (jnp.dot(hi, wlo_ref[...], preferred_element_type=F32)
                      + jnp.dot(lo, whi_ref[...], preferred_element_type=F32)))


def moe_router(x, nw, w_hi, w_lo, *, tm=256):
    m, k = x.shape
    tm = min(tm, m)
    return pl.pallas_call(
        _router_body,
        grid=(m // tm,),
        in_specs=[pl.BlockSpec((tm, k), lambda i: (i, 0)),
                  pl.BlockSpec((1, k), lambda i: (0, 0)),
                  pl.BlockSpec((k, LANES), lambda i: (0, 0)),
                  pl.BlockSpec((k, LANES), lambda i: (0, 0))],
        out_specs=[pl.BlockSpec((tm, k), lambda i: (i, 0)),
                   pl.BlockSpec((tm, LANES), lambda i: (i, 0))],
        out_shape=[jax.ShapeDtypeStruct((m, k), F32), jax.ShapeDtypeStruct((m, LANES), F32)],
        compiler_params=_params(1),
        name="moe_router",
    )(x, nw.reshape(1, k), w_hi, w_lo)


def _row_copy(src_hbm, src_row, dst, dst_row, sem):
    return pltpu.make_async_copy(src_hbm.at[pl.ds(src_row, 1)], dst.at[pl.ds(dst_row, 1)], sem)


def _experts_body(be_ref, tok_ref, nused_ref, x_hbm, wg_ref, wu_ref, wd_ref, y_ref, xbuf, sem, *, rb):
    i = pl.program_id(0)
    n_used = nused_ref[0]

    def start_gather(blk, slot):
        def body(r, carry):
            _row_copy(x_hbm, tok_ref[blk * rb + r], xbuf.at[slot], r, sem.at[slot]).start(priority=1)
            return carry
        lax.fori_loop(0, rb, body, 0, unroll=GATHER_UNROLL)

    def wait_gather(slot):
        for r in range(rb):
            _row_copy(x_hbm, 0, xbuf.at[slot], r, sem.at[slot]).wait()

    @pl.when((i == 0) & (n_used > 0))
    def _():
        start_gather(0, 0)

    @pl.when(i + 1 < n_used)
    def _():
        start_gather(i + 1, (i + 1) % 2)

    @pl.when(i < n_used)
    def _():
        slot = i % 2
        wait_gather(slot)
        x = xbuf[slot].astype(BF16)
        g = jnp.dot(x, wg_ref[...], preferred_element_type=F32)
        u = jnp.dot(x, wu_ref[...], preferred_element_type=F32)
        h = (g * jax.nn.sigmoid(g) * u).astype(BF16)
        y_ref[...] = jnp.dot(h, wd_ref[...], preferred_element_type=F32)

    @pl.when(i >= n_used)
    def _():
        y_ref[...] = jnp.zeros_like(y_ref)


def moe_experts(xn, row_tok, block_expert, n_used, w_gate, w_up, w_down, layer, *, rb):
    d = xn.shape[1]
    n_rows = row_tok.shape[0]
    de = w_gate.shape[3]
    n_blocks = n_rows // rb
    grid_spec = pltpu.PrefetchScalarGridSpec(
        num_scalar_prefetch=3,
        grid=(n_blocks,),
        in_specs=[pl.BlockSpec(memory_space=pl.ANY),
                  pl.BlockSpec((None, None, d, de), lambda i, be, tok, nu: (layer, be[i], 0, 0)),
                  pl.BlockSpec((None, None, d, de), lambda i, be, tok, nu: (layer, be[i], 0, 0)),
                  pl.BlockSpec((None, None, de, d), lambda i, be, tok, nu: (layer, be[i], 0, 0))],
        out_specs=pl.BlockSpec((rb, d), lambda i, be, tok, nu: (i, 0)),
        scratch_shapes=[pltpu.VMEM((2, rb, d), F32), pltpu.SemaphoreType.DMA((2,))],
    )
    return pl.pallas_call(
        functools.partial(_experts_body, rb=rb),
        grid_spec=grid_spec,
        out_shape=jax.ShapeDtypeStruct((n_rows, d), F32),
        compiler_params=_params(1),
        name="moe_experts",
    )(block_expert, row_tok, n_used, xn, w_gate, w_up, w_down)


def _combine_body(dest_ref, h_ref, g_ref, nw_ref, y_hbm, o_ref, buf, sem, *, tc, normalize):
    i = pl.program_id(0)

    def start_gather(tile, slot):
        def body(t, carry):
            for k in range(TOP_K):
                _row_copy(y_hbm, dest_ref[(tile * tc + t) * TOP_K + k], buf.at[slot, k], t,
                          sem.at[slot]).start(priority=k % 2)
            return carry
        lax.fori_loop(0, tc, body, 0, unroll=GATHER_UNROLL // TOP_K)

    @pl.when(i == 0)
    def _():
        start_gather(0, 0)

    @pl.when(i + 1 < pl.num_programs(0))
    def _():
        start_gather(i + 1, (i + 1) % 2)

    slot = i % 2
    for t in range(tc):
        for k in range(TOP_K):
            _row_copy(y_hbm, 0, buf.at[slot, k], t, sem.at[slot]).wait()
    g = g_ref[...]
    out = h_ref[...] + (buf[slot, 0] * g[:, 0:1] + buf[slot, 1] * g[:, 1:2])
    o_ref[...] = _rms_scale(out, nw_ref[...]) if normalize else out


def moe_combine(h, y_rows, dest, gates, out_norm_w=None, *, tc=128):
    t, d = h.shape
    tc = min(tc, t)
    normalize = out_norm_w is not None
    nw = (out_norm_w if normalize else jnp.ones((d,), F32)).reshape(1, d)
    grid_spec = pltpu.PrefetchScalarGridSpec(
        num_scalar_prefetch=1,
        grid=(t // tc,),
        in_specs=[pl.BlockSpec((tc, d), lambda i, dest: (i, 0)),
                  pl.BlockSpec((tc, TOP_K), lambda i, dest: (i, 0)),
                  pl.BlockSpec((1, d), lambda i, dest: (0, 0)),
                  pl.BlockSpec(memory_space=pl.ANY)],
        out_specs=pl.BlockSpec((tc, d), lambda i, dest: (i, 0)),
        scratch_shapes=[pltpu.VMEM((2, TOP_K, tc, d), F32), pltpu.SemaphoreType.DMA((2,))],
    )
    return pl.pallas_call(
        functools.partial(_combine_body, tc=tc, normalize=normalize),
        grid_spec=grid_spec,
        out_shape=jax.ShapeDtypeStruct((t, d), F32),
        compiler_params=_params(1),
        name="moe_combine",
    )(dest, h, gates, nw, y_rows)


def _routing_tables(logits, b_group, b_router, rb):
    t = logits.shape[0]
    group_logits = logits[:, :N_GROUPS] + b_group
    group = jnp.argmax(group_logits, axis=-1)
    p_group = jnp.take_along_axis(jax.nn.softmax(group_logits, axis=-1), group[:, None], axis=-1)[:, 0]
    exp_logits = (logits[:, N_GROUPS:N_GROUPS + N_EXPERTS] + b_router).reshape(t, N_GROUPS, EXPERTS_PER_GROUP)
    in_group = jnp.take_along_axis(exp_logits, group[:, None, None], axis=1)[:, 0]
    top_p, top_e = lax.top_k(jax.nn.softmax(in_group, axis=-1), TOP_K)
    gate = (p_group[:, None] * top_p / jnp.sum(top_p, axis=-1, keepdims=True)).reshape(-1)
    eid = (group[:, None] * EXPERTS_PER_GROUP + top_e).reshape(-1).astype(jnp.int32)

    n_assign = t * TOP_K
    seg = min(512, n_assign)
    onehot = (eid[:, None] == jnp.arange(N_EXPERTS, dtype=jnp.int32)[None, :]).astype(F32)
    within = jnp.einsum("ij,bjk->bik", jnp.tril(jnp.ones((seg, seg), F32)),
                        onehot.reshape(n_assign // seg, seg, N_EXPERTS), preferred_element_type=F32)
    seg_total = within[:, -1, :]
    seg_start = jnp.cumsum(seg_total, axis=0) - seg_total
    running = (within + seg_start[:, None, :]).reshape(n_assign, N_EXPERTS)
    counts = (seg_start[-1] + seg_total[-1]).astype(jnp.int32)
    rank = jnp.sum(running * onehot, axis=1).astype(jnp.int32) - 1
    padded = (counts + rb - 1) // rb * rb
    padded_end = jnp.cumsum(padded)
    dest = ((padded_end - padded)[eid] + rank).astype(jnp.int32)
    n_rows = n_assign + N_EXPERTS * rb
    n_blocks = n_rows // rb
    row_tok = jnp.zeros((n_rows,), jnp.int32).at[dest].set(jnp.arange(n_assign, dtype=jnp.int32) // TOP_K)
    block_row0 = jnp.arange(n_blocks, dtype=jnp.int32) * rb
    block_expert = jnp.minimum(jnp.sum((padded_end[None, :] <= block_row0[:, None]).astype(jnp.int32), axis=1),
                               N_EXPERTS - 1)
    n_used = (padded_end[-1] // rb).astype(jnp.int32).reshape(1)
    return dest, row_tok, gate.reshape(t, TOP_K), block_expert, n_used


def _pad_lanes(a):
    return jnp.pad(a, ((0, 0), (0, LANES - a.shape[1])))


def _hybrid_mixer(h, bsz, seq, layer, norm_w, w_in, w_a_bf, w_b_bf, conv_w, gla_wf2, gla_bf, gla_norm_w, fox_bf, w_out_bf):
    ff0 = w_in.shape[2] - FOX_HEADS
    w_small = _pad_lanes(jnp.concatenate([w_in[layer, :, W_IN_GLA_LR:W_IN_FOX_Q], w_in[layer, :, ff0:]],
                                         axis=1)).astype(BF16)
    xn, small = rms_cast(h, norm_w, w_small)
    proj = in_proj(xn, w_a_bf, w_b_bf, layer)

    y_conv = gated_conv(proj, conv_w, bsz, seq)

    wf_pad = jnp.pad(gla_wf2, ((SMALL_LR, LANES - SMALL_LR - GLA_RANK), (0, 0))).astype(BF16)
    y_gla = gated_linear_attention(proj, small, wf_pad, gla_bf.reshape(1, -1), gla_norm_w.reshape(1, -1), bsz, seq)

    bias_row = jnp.pad(fox_bf, (SMALL_F, LANES - SMALL_F - FOX_HEADS)).reshape(1, LANES)
    c = fox_cumulative_gate(small, bias_row, bsz, seq)
    tq = min(512, seq)
    c_blocks = (c[:, SMALL_F:SMALL_F + FOX_HEADS].reshape(bsz, seq, FOX_HEADS)
                .transpose(0, 2, 1).reshape(bsz, FOX_HEADS, seq // tq, tq))
    y_fox = forgetting_attention(proj, c_blocks, bsz, seq, tq=tq)

    return mixer_out_proj(y_conv, y_gla, y_fox, w_out_bf, layer, h)


def _cross_attention(h, bsz, seq, layer, norm_w, mem_n, w_cq, w_ck, w_cv, w_co):
    mem_len = mem_n.shape[0] // bsz
    q = matmul(rms_cast(h, norm_w), w_cq, layer)
    k = matmul(mem_n, w_ck, layer)
    v = matmul(mem_n, w_cv, layer)
    k_t = k.reshape(bsz, mem_len, D_MODEL).transpose(0, 2, 1)
    return cross_attention_out(q, k_t, v.reshape(bsz, mem_len, D_MODEL), w_co, layer, h, seq)


def _moe(h, layer, norm_w, w_group, b_group, w_router, b_router, w_gate, w_up, w_down, *, out_norm_w=None, rb=256):
    w_r = _pad_lanes(jnp.concatenate([w_group, w_router], axis=1))
    w_hi = w_r.astype(BF16)
    w_lo = (w_r - w_hi.astype(F32)).astype(BF16)
    xn, logits = moe_router(h, norm_w, w_hi, w_lo)
    dest, row_tok, gates, block_expert, n_used = _routing_tables(logits, b_group, b_router, rb)
    y_rows = moe_experts(xn, row_tok, block_expert, n_used, w_gate, w_up, w_down, layer, rb=rb)
    return moe_combine(h, y_rows, dest, gates, out_norm_w)


def kernel(x, mem, norm_mix_w, w_in, conv_w, gla_wf2, gla_bf, gla_norm_w, fox_bf, w_out, norm_cross_w, mem_norm_w, w_cq, w_ck, w_cv, w_co, norm_ffn_w, w_group, b_group, w_router, b_router, w_expert_gate, w_expert_up, w_expert_down, final_norm_w):
    bsz, seq, d = x.shape
    h = x.reshape(bsz * seq, d)
    mem_n = rms_cast(mem.reshape(-1, d), mem_norm_w)
    n_layers = norm_mix_w.shape[0]
    w_a_bf = w_in[:, :, :W_IN_GLA_LR].astype(BF16)
    w_b_bf = w_in[:, :, W_IN_FOX_Q:w_in.shape[2] - FOX_HEADS].astype(BF16)
    w_out_bf = w_out.astype(BF16)
    w_cq_bf, w_co_bf = w_cq.astype(BF16), w_co.astype(BF16)
    w_eg_bf, w_eu_bf, w_ed_bf = (w.astype(BF16) for w in (w_expert_gate, w_expert_up, w_expert_down))
    for l in range(n_layers):
        h = _hybrid_mixer(h, bsz, seq, l, norm_mix_w[l], w_in, w_a_bf, w_b_bf, conv_w[l], gla_wf2[l], gla_bf[l],
                          gla_norm_w[l], fox_bf[l], w_out_bf)
        h = _cross_attention(h, bsz, seq, l, norm_cross_w[l], mem_n, w_cq_bf, w_ck, w_cv, w_co_bf)
        last = l == n_layers - 1
        h = _moe(h, l, norm_ffn_w[l], w_group[l], b_group[l], w_router[l], b_router[l], w_eg_bf, w_eu_bf, w_ed_bf,
                 out_norm_w=final_norm_w if last else None)
    return h.reshape(bsz, seq, d)
```
